```python
import math, functools
import jax, jax.numpy as jnp
from jax import lax
import numpy as np

D_MODEL = 1024
BATCH = 1
SEQ = 16384
DEPTH = 2
DEC_BATCH = 4
DEC_SEQ = 8192
PAST_LEN = 128

GRID_W = 64
NA_HEADS = 8
NA_HEAD_DIM = 64
NA_WIDTH = NA_HEADS * NA_HEAD_DIM
NA_ROWS = 8
NA_COLS = 16
GDN_HEADS = 4
GDN_HEAD_DIM = 128
GDN_WIDTH = GDN_HEADS * GDN_HEAD_DIM
CONV_K = 5
CHUNK = 64
MIX_WIDTH = NA_WIDTH + GDN_WIDTH
IN_WIDTH = 3 * NA_WIDTH + 4 * GDN_WIDTH + 4 * GDN_HEADS
IN_SPLITS = [NA_WIDTH, 2 * NA_WIDTH, 3 * NA_WIDTH,
             3 * NA_WIDTH + 3 * GDN_WIDTH,
             3 * NA_WIDTH + 4 * GDN_WIDTH,
             3 * NA_WIDTH + 4 * GDN_WIDTH + GDN_HEADS,
             3 * NA_WIDTH + 4 * GDN_WIDTH + 2 * GDN_HEADS,
             3 * NA_WIDTH + 4 * GDN_WIDTH + 3 * GDN_HEADS]
N_EXPERTS = 16
EXPERT_FF = 1024
EC_CAPACITY = 2
EPS = 1e-6

kernel_name = 'hybrid_na_gdn_ec_encoder'


def rms_norm(x, g):
    xf = x.astype(jnp.float32)
    y = xf * lax.rsqrt(jnp.mean(xf * xf, axis=-1, keepdims=True) + EPS)
    return (y * g.astype(jnp.float32)).astype(x.dtype)


def l2norm(x):
    return x * lax.rsqrt(jnp.sum(x * x, axis=-1, keepdims=True) + EPS)


def neighbourhood_attention(q, k, v, rpb):
    B, N, H, DH = q.shape
    rows = N // GRID_W
    kr = min(NA_ROWS, rows)
    kc = NA_COLS
    grid = lambda t: t.reshape(B, rows, GRID_W, H, DH)
    qg, kg, vg = grid(q), grid(k), grid(v)
    col = jnp.arange(GRID_W)
    col_idx = jnp.clip(col - kc // 2, 0, GRID_W - kc)[:, None] + jnp.arange(kc)[None, :]
    rpb_cols = rpb[:, :, col_idx - col[:, None] + NA_COLS - 1]
    scale = DH ** -0.5

    def row_block(r):
        rs = jnp.clip(r - kr // 2, 0, rows - kr)
        kb = lax.dynamic_slice_in_dim(kg, rs, kr, axis=1)[:, :, col_idx]
        vb = lax.dynamic_slice_in_dim(vg, rs, kr, axis=1)[:, :, col_idx]
        qr = lax.dynamic_index_in_dim(qg, r, axis=1, keepdims=False)
        s = jnp.einsum('bwhd,brwkhd->bhwrk', qr, kb).astype(jnp.float32) * scale
        dr = rs + jnp.arange(kr) - r + NA_ROWS - 1
        bias = jnp.take(rpb_cols, dr, axis=1)
        s = s + jnp.transpose(bias, (0, 2, 1, 3)).astype(jnp.float32)[None]
        p = jax.nn.softmax(s.reshape(B, H, GRID_W, kr * kc), axis=-1).reshape(s.shape).astype(v.dtype)
        return jnp.einsum('bhwrk,brwkhd->bwhd', p, vb)

    out = lax.map(row_block, jnp.arange(rows))
    return jnp.moveaxis(out, 0, 1).reshape(B, N, H * DH)


def centred_conv(x, w):
    n = x.shape[1]
    pad = CONV_K // 2
    xp = jnp.pad(x, ((0, 0), (pad, CONV_K - 1 - pad), (0, 0)))
    acc = xp[:, 0:n] * w[0]
    for i in range(1, CONV_K):
        acc = acc + xp[:, i:i + n] * w[i]
    return acc


def chunked_delta_rule(q, k, v, g, beta):
    B, N, H, DK = q.shape
    DV = v.shape[-1]
    nc = N // CHUNK

    def chunks(t):
        t = t.reshape((B, nc, CHUNK, H) + t.shape[3:])
        return jnp.moveaxis(t, 3, 1)

    q = chunks(q * DK ** -0.5)
    k = chunks(k)
    v = chunks(v)
    gc = jnp.cumsum(chunks(g), axis=-1)
    beta = chunks(beta)
    k_beta = k * beta[..., None]
    v_beta = v * beta[..., None]
    idx = jnp.arange(CHUNK)
    causal = idx[:, None] >= idx[None, :]
    strict = idx[:, None] > idx[None, :]
    decay = jnp.exp(jnp.where(causal, gc[..., :, None] - gc[..., None, :], -jnp.inf))
    lower = jnp.where(strict, jnp.einsum('bhncd,bhnsd->bhncs', k_beta, k) * decay, 0.0)
    a = lower + jnp.eye(CHUNK, dtype=lower.dtype)
    solve = lambda rhs: lax.linalg.triangular_solve(a, rhs, left_side=True, lower=True, unit_diagonal=True)
    u = solve(v_beta)
    w = solve(k_beta * jnp.exp(gc)[..., None])
    qk = jnp.einsum('bhncd,bhnsd->bhncs', q, k) * decay

    def step(S, xs):
        q_i, k_i, u_i, w_i, qk_i, g_i = xs
        v_new = u_i - jnp.einsum('bhcd,bhde->bhce', w_i, S)
        o = (jnp.einsum('bhcd,bhde->bhce', q_i * jnp.exp(g_i)[..., None], S)
             + jnp.einsum('bhcs,bhse->bhce', qk_i, v_new))
        g_last = g_i[..., -1]
        S = (S * jnp.exp(g_last)[..., None, None]
             + jnp.einsum('bhcd,bhce->bhde', k_i * jnp.exp(g_last[..., None] - g_i)[..., None], v_new))
        return S, o

    xs = tuple(jnp.moveaxis(t, 2, 0) for t in (q, k, u, w, qk, gc))
    S0 = jnp.zeros((B, H, DK, DV), q.dtype)
    _, o = lax.scan(step, S0, xs)
    return jnp.transpose(o, (1, 0, 3, 2, 4)).reshape(B, N, H, DV)


def gated_deltanet(qkv, z, b_f, b_b, a_f, a_b, conv_w, a_log, dt_bias, norm_w):
    B, N, _ = qkv.shape
    c = jax.nn.silu(centred_conv(qkv, conv_w).astype(jnp.float32))
    q, k, v = jnp.split(c, 3, axis=-1)
    heads = lambda t: t.reshape(B, N, GDN_HEADS, GDN_HEAD_DIM)
    q, k, v = l2norm(heads(q)), l2norm(heads(k)), heads(v)

    def gates(b, a, d):
        beta = jax.nn.sigmoid(b.astype(jnp.float32))
        g = -jnp.exp(a_log[d].astype(jnp.float32)) * jax.nn.softplus(a.astype(jnp.float32) + dt_bias[d].astype(jnp.float32))
        return g, beta

    g_f, beta_f = gates(b_f, a_f, 0)
    g_b, beta_b = gates(b_b, a_b, 1)
    flip = lambda t: jnp.flip(t, axis=1)
    o_f = chunked_delta_rule(q, k, v, g_f, beta_f)
    o_b = flip(chunked_delta_rule(flip(q), flip(k), flip(v), flip(g_b), flip(beta_b)))
    o = o_f + o_b
    o = o * lax.rsqrt(jnp.mean(o * o, axis=-1, keepdims=True) + EPS) * norm_w.astype(jnp.float32)
    o = o * jax.nn.silu(z.astype(jnp.float32).reshape(B, N, GDN_HEADS, GDN_HEAD_DIM))
    return o.reshape(B, N, GDN_WIDTH).astype(qkv.dtype)


def expert_choice_ffn(h, w_router, w_gate, w_up, w_down):
    B, N, D = h.shape
    n = B * N
    cap = EC_CAPACITY * n // N_EXPERTS
    t = h.reshape(n, D)
    aff = jax.nn.softmax((t @ w_router).astype(jnp.float32), axis=-1)
    gates, idx = lax.top_k(aff.T, cap)
    xe = t[idx]
    hid = jax.nn.silu(jnp.einsum('ecd,edf->ecf', xe, w_gate)) * jnp.einsum('ecd,edf->ecf', xe, w_up)
    ye = jnp.einsum('ecf,efd->ecd', hid, w_down) * gates[..., None].astype(h.dtype)
    y = jnp.zeros((n, D), h.dtype).at[idx.reshape(-1)].add(ye.reshape(-1, D))
    return y.reshape(B, N, D)


def encoder_layer(x, ln1, w_in, conv_w, a_log, dt_bias, gdn_norm, rpb, w_out,
                  ln2, w_router, w_gate, w_up, w_down):
    B, N, _ = x.shape
    h = rms_norm(x, ln1)
    proj = h @ w_in
    na_q, na_k, na_v, gdn_qkv, z, b_f, b_b, a_f, a_b = jnp.split(proj, IN_SPLITS, axis=-1)
    na_heads = lambda t: t.reshape(B, N, NA_HEADS, NA_HEAD_DIM)
    na_out = neighbourhood_attention(na_heads(na_q), na_heads(na_k), na_heads(na_v), rpb)
    gdn_out = gated_deltanet(gdn_qkv, z, b_f, b_b, a_f, a_b, conv_w, a_log, dt_bias, gdn_norm)
    x = x + jnp.concatenate([na_out, gdn_out], axis=-1) @ w_out
    x = x + expert_choice_ffn(rms_norm(x, ln2), w_router, w_gate, w_up, w_down)
    return x


def trunk(x, ln1, w_in, conv_w, a_log, dt_bias, gdn_norm, rpb, w_out,
          ln2, w_router, w_gate, w_up, w_down, ln_f):
    for l in range(DEPTH):
        x = encoder_layer(x, ln1[l], w_in[l], conv_w[l], a_log[l], dt_bias[l], gdn_norm[l],
                          rpb[l], w_out[l], ln2[l], w_router[l], w_gate[l], w_up[l], w_down[l])
    return rms_norm(x, ln_f)


def setup_inputs(seed: int = 0) -> dict:
    key = jax.random.key(seed)
    ks = jax.random.split(key, 18)
    f32 = jnp.float32
    nrm = lambda k, s: jax.random.normal(k, s, f32)
    dt = jnp.exp(jax.random.uniform(ks[5], (DEPTH, 2, GDN_HEADS), f32, math.log(0.001), math.log(0.1)))
    return {
        'x_prompt': nrm(ks[0], (BATCH, SEQ, D_MODEL)),
        'x_sample': nrm(ks[1], (DEC_BATCH, DEC_SEQ, D_MODEL)),
        'ln1': 1.0 + 0.02 * nrm(ks[2], (DEPTH, D_MODEL)),
        'w_in': nrm(ks[3], (DEPTH, D_MODEL, IN_WIDTH)) * D_MODEL ** -0.5,
        'conv_w': nrm(ks[4], (DEPTH, CONV_K, 3 * GDN_WIDTH)) * CONV_K ** -0.5,
        'a_log': jnp.log(jax.random.uniform(ks[6], (DEPTH, 2, GDN_HEADS), f32, 1.0, 16.0)),
        'dt_bias': dt + jnp.log(-jnp.expm1(-dt)),
        'gdn_norm': 1.0 + 0.02 * nrm(ks[7], (DEPTH, GDN_HEAD_DIM)),
        'rpb': 0.1 * nrm(ks[8], (DEPTH, NA_HEADS, 2 * NA_ROWS - 1, 2 * NA_COLS - 1)),
        'w_out': nrm(ks[9], (DEPTH, MIX_WIDTH, D_MODEL)) * MIX_WIDTH ** -0.5,
        'ln2': 1.0 + 0.02 * nrm(ks[10], (DEPTH, D_MODEL)),
        'w_router': nrm(ks[11], (DEPTH, D_MODEL, N_EXPERTS)) * D_MODEL ** -0.5,
        'w_gate': nrm(ks[12], (DEPTH, N_EXPERTS, D_MODEL, EXPERT_FF)) * D_MODEL ** -0.5,
        'w_up': nrm(ks[13], (DEPTH, N_EXPERTS, D_MODEL, EXPERT_FF)) * D_MODEL ** -0.5,
        'w_down': nrm(ks[14], (DEPTH, N_EXPERTS, EXPERT_FF, D_MODEL)) * EXPERT_FF ** -0.5,
        'ln_f': 1.0 + 0.02 * nrm(ks[15], (D_MODEL,)),
    }


def reference(x_prompt, x_sample, ln1, w_in, conv_w, a_log, dt_bias, gdn_norm, rpb, w_out,
              ln2, w_router, w_gate, w_up, w_down, ln_f):
    y_prompt = trunk(x_prompt, ln1, w_in, conv_w, a_log, dt_bias, gdn_norm, rpb, w_out,
                     ln2, w_router, w_gate, w_up, w_down, ln_f)
    y_sample = trunk(x_sample, ln1, w_in, conv_w, a_log, dt_bias, gdn_norm, rpb, w_out,
                     ln2, w_router, w_gate, w_up, w_down, ln_f)
    return (y_prompt, y_sample)
```

```python
import functools
import math

import jax
import jax.numpy as jnp
from jax import lax
from jax.experimental import pallas as pl
from jax.experimental.pallas import tpu as pltpu

F32 = jnp.float32
BF16 = jnp.bfloat16

D_MODEL = 1024
GRID_W = 64
NA_HEADS = 8
NA_HEAD_DIM = 64
NA_WIDTH = 512
NA_ROWS = 8
NA_COLS = 16
GDN_HEADS = 4
GDN_HEAD_DIM = 128
GDN_WIDTH = 512
CONV_K = 5
CHUNK = 64
IN_WIDTH = 3600
N_EXPERTS = 16
EXPERT_FF = 1024
EC_CAPACITY = 2
EPS = 1e-6

LANES = 128
IN_WIDTH_PAD = 3712
GATE_COL0 = 3584
NEG_BIG = -1e30
VMEM_LIMIT = 56 * 1024 * 1024

ROW_TILE = 512
NA_ROW_BLOCK = 8
GDN_BLOCK_CHUNKS = 2
FFN_ROW_TILE = 512


def _cparams(sem):
    return pltpu.CompilerParams(dimension_semantics=sem, vmem_limit_bytes=VMEM_LIMIT)


def _bdot(a, b):
    return jnp.dot(a.astype(BF16), b.astype(BF16), preferred_element_type=F32)


def _bdot_nt(a, b):
    return lax.dot_general(a.astype(BF16), b.astype(BF16), (((1,), (1,)), ((), ())),
                           preferred_element_type=F32)


def _bdot_tn(a, b):
    return lax.dot_general(a.astype(BF16), b.astype(BF16), (((0,), (0,)), ((), ())),
                           preferred_element_type=F32)


def _split3(x):
    x1 = x.astype(BF16)
    r1 = x - x1.astype(F32)
    x2 = r1.astype(BF16)
    r2 = r1 - x2.astype(F32)
    return x1, x2, r2.astype(BF16)


def _silu(x):
    return x * (1.0 / (1.0 + jnp.exp(-x)))


def _sigmoid(x):
    return 1.0 / (1.0 + jnp.exp(-x))


def _softplus(x):
    return jnp.maximum(x, 0.0) + jnp.log(1.0 + jnp.exp(-jnp.abs(x)))


def _rms(x, g):
    ms = jnp.mean(x * x, axis=-1, keepdims=True)
    return (x * lax.rsqrt(ms + EPS)) * g


def _inproj_kernel(has_res, *refs):
    if has_res:
        xa_ref, xb_ref, ln_ref, w_ref, wgt_ref, x_out, q_out, k_out, v_out, c_out, z_out, g_out, gt_out = refs
        x = xa_ref[...] + xb_ref[...]
        x_out[...] = x
    else:
        xa_ref, ln_ref, w_ref, wgt_ref, q_out, k_out, v_out, c_out, z_out, g_out, gt_out = refs
        x = xa_ref[...]
    h = _rms(x, ln_ref[...]).astype(BF16)
    q_out[...] = jnp.dot(h, w_ref[:, 0:512], preferred_element_type=F32).astype(BF16)
    k_out[...] = jnp.dot(h, w_ref[:, 512:1024], preferred_element_type=F32).astype(BF16)
    v_out[...] = jnp.dot(h, w_ref[:, 1024:1536], preferred_element_type=F32).astype(BF16)
    c_out[...] = jnp.dot(h, w_ref[:, 1536:3072], preferred_element_type=F32)
    z_out[...] = jnp.dot(h, w_ref[:, 3072:3584], preferred_element_type=F32)
    g_out[...] = jnp.dot(h, w_ref[:, GATE_COL0:IN_WIDTH_PAD], preferred_element_type=F32)
    gt_out[...] = lax.dot_general(wgt_ref[...], h, (((1,), (1,)), ((), ())), preferred_element_type=F32)


def _inproj(xa, xb, ln, w_pad, wgt):
    n = xa.shape[0]
    tm = ROW_TILE
    has_res = xb is not None
    row = lambda w: pl.BlockSpec((tm, w), lambda i: (i, 0))
    const = lambda s: pl.BlockSpec(s, lambda i: (0, 0))
    in_specs = [row(D_MODEL)] + ([row(D_MODEL)] if has_res else []) + [
        const((1, D_MODEL)), const((D_MODEL, IN_WIDTH_PAD)), const((16, D_MODEL))]
    out_shape = ([jax.ShapeDtypeStruct((n, D_MODEL), F32)] if has_res else []) + [
        jax.ShapeDtypeStruct((n, NA_WIDTH), BF16), jax.ShapeDtypeStruct((n, NA_WIDTH), BF16),
        jax.ShapeDtypeStruct((n, NA_WIDTH), BF16), jax.ShapeDtypeStruct((n, 3 * GDN_WIDTH), F32),
        jax.ShapeDtypeStruct((n, GDN_WIDTH), F32), jax.ShapeDtypeStruct((n, LANES), F32),
        jax.ShapeDtypeStruct((16, n), F32)]
    out_specs = ([row(D_MODEL)] if has_res else []) + [
        row(NA_WIDTH), row(NA_WIDTH), row(NA_WIDTH), row(3 * GDN_WIDTH), row(GDN_WIDTH), row(LANES),
        pl.BlockSpec((16, tm), lambda i: (0, i))]
    args = (xa,) + ((xb,) if has_res else ()) + (ln, w_pad, wgt)
    outs = pl.pallas_call(
        functools.partial(_inproj_kernel, has_res), grid=(n // tm,), in_specs=in_specs,
        out_specs=out_specs, out_shape=out_shape, compiler_params=_cparams(("parallel",)),
        name="inproj_res" if has_res else "inproj")(*args)
    if has_res:
        return outs[0], outs[1:]
    return xa, outs


def _na_kernel(rows, q_ref, k_ref, v_ref, bias_ref, o_ref):
    i = pl.program_id(2)
    lane = lax.broadcasted_iota(jnp.int32, (GRID_W, LANES), 1)
    low = lane < NA_HEAD_DIM
    scale = NA_HEAD_DIM ** -0.5
    for rr in range(NA_ROW_BLOCK):
        r = i * NA_ROW_BLOCK + rr
        rs = jnp.clip(r - NA_ROWS // 2, 0, rows - NA_ROWS)
        d0 = rs - r + NA_ROWS - 1
        k0 = pl.multiple_of(rs * GRID_W, GRID_W)
        kw = k_ref[0, pl.ds(k0, NA_ROWS * GRID_W), :]
        vw = v_ref[0, pl.ds(k0, NA_ROWS * GRID_W), :]
        q = q_ref[0, rr * GRID_W:(rr + 1) * GRID_W, :]
        outs = []
        for hh in range(2):
            qm = jnp.where(low if hh == 0 else jnp.logical_not(low), q, jnp.zeros_like(q))
            s = _bdot_nt(qm, kw) * scale + bias_ref[hh, d0]
            m = jnp.max(s, axis=-1, keepdims=True)
            p = jnp.exp(s - m)
            p = p / jnp.sum(p, axis=-1, keepdims=True)
            outs.append(_bdot(p, vw))
        o_ref[0, rr * GRID_W:(rr + 1) * GRID_W, :] = jnp.where(low, outs[0], outs[1]).astype(o_ref.dtype)


def _na_bias_table(rpb):
    col = jnp.arange(GRID_W)
    cs = jnp.clip(col - NA_COLS // 2, 0, GRID_W - NA_COLS)
    kcol = jnp.arange(GRID_W)
    valid = (kcol[None, :] >= cs[:, None]) & (kcol[None, :] < cs[:, None] + NA_COLS)
    cidx = jnp.clip(kcol[None, :] - col[:, None] + NA_COLS - 1, 0, 2 * NA_COLS - 2)
    band = jnp.where(valid[None, None], rpb[:, :, cidx], NEG_BIG)
    ridx = jnp.arange(NA_ROWS)[:, None] + jnp.arange(NA_ROWS)[None, :]
    tab = band[:, ridx]
    tab = jnp.transpose(tab, (0, 1, 3, 2, 4))
    return tab.reshape(NA_HEADS, NA_ROWS, GRID_W, NA_ROWS * GRID_W).astype(F32)


def _neighbourhood_attention(q, k, v, bias_tab, batch):
    n = q.shape[0]
    seq = n // batch
    rows = seq // GRID_W
    q3, k3, v3 = (t.reshape(batch, seq, NA_WIDTH) for t in (q, k, v))
    tq = NA_ROW_BLOCK * GRID_W
    out = pl.pallas_call(
        functools.partial(_na_kernel, rows),
        grid=(batch, NA_HEADS // 2, rows // NA_ROW_BLOCK),
        in_specs=[pl.BlockSpec((1, tq, LANES), lambda b, hp, i: (b, i, hp)),
                  pl.BlockSpec((1, seq, LANES), lambda b, hp, i: (b, 0, hp)),
                  pl.BlockSpec((1, seq, LANES), lambda b, hp, i: (b, 0, hp)),
                  pl.BlockSpec((2, NA_ROWS, GRID_W, NA_ROWS * GRID_W), lambda b, hp, i: (hp, 0, 0, 0))],
        out_specs=pl.BlockSpec((1, tq, LANES), lambda b, hp, i: (b, i, hp)),
        out_shape=jax.ShapeDtypeStruct((batch, seq, NA_WIDTH), BF16),
        compiler_params=_cparams(("parallel", "parallel", "arbitrary")),
        name="na_attention")(q3, k3, v3, bias_tab)
    return out.reshape(n, NA_WIDTH)


def _gdn_prep_kernel(nblk, c_ref, prev_ref, next_ref, w_ref, q_out, k_out, v_out):
    i = pl.program_id(1)
    tm = c_ref.shape[1]
    pad = CONV_K // 2
    prev = jnp.where(i > 0, prev_ref[0], 0.0)
    nxt = jnp.where(i < nblk - 1, next_ref[0], 0.0)
    xc = jnp.concatenate([prev, c_ref[0], nxt], axis=0)
    acc = xc[8 - pad:8 - pad + tm] * w_ref[0:1, :]
    for j in range(1, CONV_K):
        acc = acc + xc[8 - pad + j:8 - pad + j + tm] * w_ref[j:j + 1, :]
    c = _silu(acc)
    for hd in range(GDN_HEADS):
        sl = slice(hd * GDN_HEAD_DIM, (hd + 1) * GDN_HEAD_DIM)
        qh = c[:, sl]
        kh = c[:, GDN_WIDTH + hd * GDN_HEAD_DIM:GDN_WIDTH + (hd + 1) * GDN_HEAD_DIM]
        q_out[0, :, sl] = qh * lax.rsqrt(jnp.sum(qh * qh, axis=-1, keepdims=True) + EPS)
        k_out[0, :, sl] = kh * lax.rsqrt(jnp.sum(kh * kh, axis=-1, keepdims=True) + EPS)
    v_out[0] = c[:, 2 * GDN_WIDTH:]


def _gdn_prep(c, conv_w, batch):
    n = c.shape[0]
    seq = n // batch
    tm = ROW_TILE
    nblk = seq // tm
    c3 = c.reshape(batch, seq, 3 * GDN_WIDTH)
    hb = tm // 8
    spec = pl.BlockSpec((1, tm, GDN_WIDTH), lambda b, i: (b, i, 0))
    outs = pl.pallas_call(
        functools.partial(_gdn_prep_kernel, nblk),
        grid=(batch, nblk),
        in_specs=[pl.BlockSpec((1, tm, 3 * GDN_WIDTH), lambda b, i: (b, i, 0)),
                  pl.BlockSpec((1, 8, 3 * GDN_WIDTH), lambda b, i: (b, jnp.maximum(i * hb - 1, 0), 0)),
                  pl.BlockSpec((1, 8, 3 * GDN_WIDTH), lambda b, i: (b, jnp.minimum((i + 1) * hb, seq // 8 - 1), 0)),
                  pl.BlockSpec((8, 3 * GDN_WIDTH), lambda b, i: (0, 0))],
        out_specs=[spec, spec, spec],
        out_shape=[jax.ShapeDtypeStruct((batch, seq, GDN_WIDTH), F32)] * 3,
        compiler_params=_cparams(("parallel", "parallel")),
        name="gdn_prep")(c3, c3, c3, conv_w)
    return tuple(t.reshape(n, GDN_WIDTH) for t in outs)


def _delta_chunk(d, q, k, v, graw, gtraw, arow, drow, acol, dcol, s_ref):
    ri = lax.broadcasted_iota(jnp.int32, (CHUNK, CHUNK), 0)
    ci = lax.broadcasted_iota(jnp.int32, (CHUNK, CHUNK), 1)
    if d == 0:
        incl, strict = ri >= ci, ri > ci
    else:
        incl, strict = ri <= ci, ri < ci
    tri = jnp.where(incl, 1.0, 0.0).astype(BF16)
    tri_t = jnp.where(incl, 0.0, 1.0).astype(BF16) + jnp.where(ri == ci, 1.0, 0.0).astype(BF16)
    eye = jnp.where(ri == ci, 1.0, 0.0)

    g_full = -jnp.exp(arow) * _softplus(graw + drow)
    beta_full = _sigmoid(graw)
    gt_full = -jnp.exp(acol) * _softplus(gtraw + dcol)
    g1, g2, g3 = _split3(g_full)
    gc_full = (jnp.dot(tri, g1, preferred_element_type=F32) + jnp.dot(tri, g2, preferred_element_type=F32)
               + jnp.dot(tri, g3, preferred_element_type=F32))
    t1, t2, t3 = _split3(gt_full)
    gct_full = (jnp.dot(t1, tri_t, preferred_element_type=F32) + jnp.dot(t2, tri_t, preferred_element_type=F32)
                + jnp.dot(t3, tri_t, preferred_element_type=F32))
    last = CHUNK - 1 if d == 0 else 0
    outs = []
    for hd in range(GDN_HEADS):
        sl = slice(hd * GDN_HEAD_DIM, (hd + 1) * GDN_HEAD_DIM)
        gl = 8 + 4 * d + hd
        bl = 4 * d + hd
        gc = gc_full[:, gl:gl + 1]
        gr = gct_full[gl:gl + 1, :]
        beta = beta_full[:, bl:bl + 1]
        qh = q[:, sl] * (GDN_HEAD_DIM ** -0.5)
        kh = k[:, sl]
        vh = v[:, sl]
        diff = gc - gr
        decay = jnp.where(incl, jnp.exp(jnp.where(incl, diff, 0.0)), 0.0)
        qkk = _bdot_nt(jnp.concatenate([qh, kh], axis=0), kh)
        qk = qkk[:CHUNK] * decay
        lmat = jnp.where(strict, qkk[CHUNK:] * beta * decay, 0.0)
        tinv = eye - lmat
        pw = lmat
        for _ in range(5):
            pw = _bdot(pw, pw)
            tinv = tinv + _bdot(tinv, pw)
        egc = jnp.exp(gc)
        rhs = jnp.concatenate([vh * beta, kh * (beta * egc)], axis=1)
        uw = _bdot(tinv, rhs)
        u, w = uw[:, :GDN_HEAD_DIM], uw[:, GDN_HEAD_DIM:]
        s_old = s_ref[d, hd]
        ws_qs = _bdot(jnp.concatenate([w, qh * egc], axis=0), s_old)
        v_new = u - ws_qs[:CHUNK]
        outs.append(ws_qs[CHUNK:] + _bdot(qk, v_new))
        g_last = gc[last:last + 1, :]
        kg = kh * jnp.exp(g_last - gc)
        s_ref[d, hd] = s_old * jnp.exp(g_last) + _bdot_tn(kg, v_new)
    return jnp.concatenate(outs, axis=1)


def _gdn_scan_kernel(qf, kf, vf, gf, gtf, qb, kb, vb, gb, gtb, arow, drow, acol, dcol, of_ref, ob_ref, s_ref):
    @pl.when(pl.program_id(1) == 0)
    def _():
        s_ref[...] = jnp.zeros_like(s_ref)

    ar, dr, ac, dc = arow[...], drow[...], acol[...], dcol[...]
    for c in range(GDN_BLOCK_CHUNKS):
        sl = slice(c * CHUNK, (c + 1) * CHUNK)
        of_ref[0, sl, :] = _delta_chunk(0, qf[0, sl, :], kf[0, sl, :], vf[0, sl, :], gf[0, sl, :], gtf[0, c],
                                        ar, dr, ac, dc, s_ref)
        cb = GDN_BLOCK_CHUNKS - 1 - c
        slb = slice(cb * CHUNK, (cb + 1) * CHUNK)
        ob_ref[0, slb, :] = _delta_chunk(1, qb[0, slb, :], kb[0, slb, :], vb[0, slb, :], gb[0, slb, :], gtb[0, cb],
                                         ar, dr, ac, dc, s_ref)


def _gdn_scan(q, k, v, gate, gate_t, a_log, dt_bias, batch):
    n = q.shape[0]
    seq = n // batch
    tb = GDN_BLOCK_CHUNKS * CHUNK
    nblk = seq // tb
    q3, k3, v3 = (t.reshape(batch, seq, GDN_WIDTH) for t in (q, k, v))
    g3 = gate.reshape(batch, seq, LANES)
    gt4 = jnp.transpose(gate_t.reshape(16, batch, seq // CHUNK, CHUNK), (1, 2, 0, 3))
    arow = jnp.zeros((1, LANES), F32).at[0, 8:16].set(a_log.reshape(-1))
    drow = jnp.zeros((1, LANES), F32).at[0, 8:16].set(dt_bias.reshape(-1))
    acol = arow[0, :16].reshape(16, 1)
    dcol = drow[0, :16].reshape(16, 1)
    fwd = lambda w: pl.BlockSpec((1, tb, w), lambda b, i: (b, i, 0))
    bwd = lambda w: pl.BlockSpec((1, tb, w), lambda b, i: (b, nblk - 1 - i, 0))
    gtf = pl.BlockSpec((1, GDN_BLOCK_CHUNKS, 16, CHUNK), lambda b, i: (b, i, 0, 0))
    gtb = pl.BlockSpec((1, GDN_BLOCK_CHUNKS, 16, CHUNK), lambda b, i: (b, nblk - 1 - i, 0, 0))
    const = lambda s: pl.BlockSpec(s, lambda b, i: (0, 0))
    o_f, o_b = pl.pallas_call(
        _gdn_scan_kernel, grid=(batch, nblk),
        in_specs=[fwd(GDN_WIDTH), fwd(GDN_WIDTH), fwd(GDN_WIDTH), fwd(LANES), gtf,
                  bwd(GDN_WIDTH), bwd(GDN_WIDTH), bwd(GDN_WIDTH), bwd(LANES), gtb,
                  const((1, LANES)), const((1, LANES)), const((16, 1)), const((16, 1))],
        out_specs=[fwd(GDN_WIDTH), bwd(GDN_WIDTH)],
        out_shape=[jax.ShapeDtypeStruct((batch, seq, GDN_WIDTH), F32)] * 2,
        scratch_shapes=[pltpu.VMEM((2, GDN_HEADS, GDN_HEAD_DIM, GDN_HEAD_DIM), F32)],
        compiler_params=_cparams(("parallel", "arbitrary")),
        name="gdn_scan")(q3, k3, v3, g3, gt4, q3, k3, v3, g3, gt4, arow, drow, acol, dcol)
    return o_f.reshape(n, GDN_WIDTH), o_b.reshape(n, GDN_WIDTH)


def _outproj_kernel(x_ref, na_ref, of_ref, ob_ref, z_ref, gn_ref, w_ref, ln_ref, wr_ref, x1_out, h_out, aff_out):
    o = of_ref[...] + ob_ref[...]
    z = z_ref[...]
    gn = gn_ref[...]
    acc = jnp.dot(na_ref[...], w_ref[0:NA_WIDTH, :], preferred_element_type=F32)
    for hd in range(GDN_HEADS):
        sl = slice(hd * GDN_HEAD_DIM, (hd + 1) * GDN_HEAD_DIM)
        oh = o[:, sl]
        oh = oh * lax.rsqrt(jnp.mean(oh * oh, axis=-1, keepdims=True) + EPS) * gn
        oh = oh * _silu(z[:, sl])
        acc = acc + jnp.dot(oh.astype(BF16), w_ref[NA_WIDTH + hd * GDN_HEAD_DIM:NA_WIDTH + (hd + 1) * GDN_HEAD_DIM, :],
                            preferred_element_type=F32)
    x1 = x_ref[...] + acc
    x1_out[...] = x1
    h = _rms(x1, ln_ref[...])
    h_out[...] = h.astype(BF16)
    h1, h2, h3 = _split3(h)
    w1, w2, w3 = _split3(wr_ref[...])
    nt = lambda a, b: lax.dot_general(a, b, (((1,), (1,)), ((), ())), preferred_element_type=F32)
    logits = (nt(w1, h1) + (nt(w1, h2) + nt(w2, h1)) + (nt(w1, h3) + nt(w2, h2) + nt(w3, h1)))
    m = jnp.max(logits, axis=0, keepdims=True)
    e = jnp.exp(logits - m)
    aff_out[...] = e / jnp.sum(e, axis=0, keepdims=True)


def _outproj(x, na, o_f, o_b, z, gdn_norm, w_out, ln2, w_router_t):
    n = x.shape[0]
    tm = ROW_TILE
    row = lambda w: pl.BlockSpec((tm, w), lambda i: (i, 0))
    const = lambda s: pl.BlockSpec(s, lambda i: (0, 0))
    return pl.pallas_call(
        _outproj_kernel, grid=(n // tm,),
        in_specs=[row(D_MODEL), row(NA_WIDTH), row(GDN_WIDTH), row(GDN_WIDTH), row(GDN_WIDTH),
                  const((1, GDN_HEAD_DIM)), const((D_MODEL, D_MODEL)), const((1, D_MODEL)), const((N_EXPERTS, D_MODEL))],
        out_specs=[row(D_MODEL), row(D_MODEL), pl.BlockSpec((N_EXPERTS, tm), lambda i: (0, i))],
        out_shape=[jax.ShapeDtypeStruct((n, D_MODEL), F32), jax.ShapeDtypeStruct((n, D_MODEL), BF16),
                   jax.ShapeDtypeStruct((N_EXPERTS, n), F32)],
        compiler_params=_cparams(("parallel",)),
        name="outproj_router")(x, na, o_f, o_b, z, gdn_norm, w_out, ln2, w_router_t)


def _ffn_kernel(x_ref, g_ref, wg_ref, wu_ref, wd_ref, y_out):
    x = x_ref[0]
    a = jnp.dot(x, wg_ref[0], preferred_element_type=F32)
    b = jnp.dot(x, wu_ref[0], preferred_element_type=F32)
    hid = (_silu(a) * b).astype(BF16)
    y_out[0] = jnp.dot(hid, wd_ref[0], preferred_element_type=F32) * g_ref[0]


def _expert_ffn(xe, gates, w_gate, w_up, w_down):
    e, r, _ = xe.shape
    tm = FFN_ROW_TILE
    wspec = pl.BlockSpec((1, D_MODEL, EXPERT_FF), lambda ei, j: (ei, 0, 0))
    return pl.pallas_call(
        _ffn_kernel, grid=(e, r // tm),
        in_specs=[pl.BlockSpec((1, tm, D_MODEL), lambda ei, j: (ei, j, 0)),
                  pl.BlockSpec((1, tm, 1), lambda ei, j: (ei, j, 0)),
                  wspec, wspec, pl.BlockSpec((1, EXPERT_FF, D_MODEL), lambda ei, j: (ei, 0, 0))],
        out_specs=pl.BlockSpec((1, tm, D_MODEL), lambda ei, j: (ei, j, 0)),
        out_shape=jax.ShapeDtypeStruct((e, r, D_MODEL), F32),
        compiler_params=_cparams(("parallel", "arbitrary")),
        name="expert_ffn")(xe, gates, w_gate, w_up, w_down)


def _final_kernel(xa_ref, xb_ref, ln_ref, o_ref):
    o_ref[...] = _rms(xa_ref[...] + xb_ref[...], ln_ref[...])


def _final_norm(xa, xb, ln):
    n = xa.shape[0]
    tm = ROW_TILE
    row = pl.BlockSpec((tm, D_MODEL), lambda i: (i, 0))
    return pl.pallas_call(
        _final_kernel, grid=(n // tm,), in_specs=[row, row, pl.BlockSpec((1, D_MODEL), lambda i: (0, 0))],
        out_specs=row, out_shape=jax.ShapeDtypeStruct((n, D_MODEL), F32),
        compiler_params=_cparams(("parallel",)), name="final_norm")(xa, xb, ln)


def _moe(h_bf, aff_t, wg, wu, wd):
    n = h_bf.shape[0]
    cap = EC_CAPACITY * n // N_EXPERTS
    gates, idx = lax.top_k(aff_t, cap)
    xe = jnp.take(h_bf, idx, axis=0)
    ye = _expert_ffn(xe, gates[..., None], wg, wu, wd)
    return jnp.zeros((n, D_MODEL), F32).at[idx.reshape(-1)].add(ye.reshape(-1, D_MODEL))


def _prep_layer(l, ln1, w_in, conv_w, a_log, dt_bias, gdn_norm, rpb, w_out, ln2, w_router, w_gate, w_up, w_down):
    w_pad = jnp.pad(w_in[l], ((0, 0), (0, IN_WIDTH_PAD - IN_WIDTH))).astype(BF16)
    return dict(
        ln1=ln1[l].reshape(1, D_MODEL), w_pad=w_pad,
        wgt=jnp.transpose(w_in[l][:, GATE_COL0:IN_WIDTH]).astype(BF16),
        conv_w=jnp.pad(conv_w[l], ((0, 8 - CONV_K), (0, 0))),
        a_log=a_log[l], dt_bias=dt_bias[l], gdn_norm=gdn_norm[l].reshape(1, GDN_HEAD_DIM),
        bias_tab=_na_bias_table(rpb[l]), w_out=w_out[l].astype(BF16), ln2=ln2[l].reshape(1, D_MODEL),
        w_router_t=jnp.transpose(w_router[l]),
        wg=w_gate[l].astype(BF16), wu=w_up[l].astype(BF16), wd=w_down[l].astype(BF16))


def _trunk(x3, layers, ln_f):
    batch, seq, _ = x3.shape
    xa = x3.reshape(batch * seq, D_MODEL)
    xb = None
    for p in layers:
        x, (q, k, v, c, z, gate, gate_t) = _inproj(xa, xb, p["ln1"], p["w_pad"], p["wgt"])
        na = _neighbourhood_attention(q, k, v, p["bias_tab"], batch)
        gq, gk, gv = _gdn_prep(c, p["conv_w"], batch)
        o_f, o_b = _gdn_scan(gq, gk, gv, gate, gate_t, p["a_log"], p["dt_bias"], batch)
        x1, h_bf, aff_t = _outproj(x, na, o_f, o_b, z, p["gdn_norm"], p["w_out"], p["ln2"], p["w_router_t"])
        xa, xb = x1, _moe(h_bf, aff_t, p["wg"], p["wu"], p["wd"])
    return _final_norm(xa, xb, ln_f.reshape(1, D_MODEL)).reshape(batch, seq, D_MODEL)


def kernel(x_prompt, x_sample, ln1, w_in, conv_w, a_log, dt_bias, gdn_norm, rpb, w_out, ln2, w_router, w_gate, w_up, w_down, ln_f):
    layers = [_prep_layer(l, ln1, w_in, conv_w, a_log, dt_bias, gdn_norm, rpb, w_out, ln2, w_router,
                          w_gate, w_up, w_down) for l in range(ln1.shape[0])]
    return (_trunk(x_prompt, layers, ln_f), _trunk(x_sample, layers, ln_f))
```

```python
import functools

import jax
import jax.numpy as jnp
from jax import lax
from jax.experimental import pallas as pl
from jax.experimental.pallas import tpu as pltpu

F32 = jnp.float32
BF16 = jnp.bfloat16

D_MODEL = 1024
GRID_W = 64
NA_HEADS = 8
NA_HEAD_DIM = 64
NA_WIDTH = 512
NA_ROWS = 8
NA_COLS = 16
GDN_HEADS = 4
GDN_HEAD_DIM = 128
GDN_WIDTH = 512
CONV_K = 5
CHUNK = 64
IN_WIDTH = 3600
N_EXPERTS = 16
EXPERT_FF = 1024
EC_CAPACITY = 2
EPS = 1e-6

LANES = 128
IN_WIDTH_PAD = 3712
GATE_COL0 = 3584
NEG_BIG = -1e30
VMEM_LIMIT = 56 * 1024 * 1024

ROW_TILE = 512
NA_ROW_BLOCK = 8
GDN_BLOCK_CHUNKS = 4
FFN_ROW_TILE = 512


def _cparams(sem):
    return pltpu.CompilerParams(dimension_semantics=sem, vmem_limit_bytes=VMEM_LIMIT)


def _bdot(a, b):
    return jnp.dot(a.astype(BF16), b.astype(BF16), preferred_element_type=F32)


def _bdot_nt(a, b):
    return lax.dot_general(a.astype(BF16), b.astype(BF16), (((1,), (1,)), ((), ())),
                           preferred_element_type=F32)


def _bdot_tn(a, b):
    return lax.dot_general(a.astype(BF16), b.astype(BF16), (((0,), (0,)), ((), ())),
                           preferred_element_type=F32)


def _split3(x):
    x1 = x.astype(BF16)
    r1 = x - x1.astype(F32)
    x2 = r1.astype(BF16)
    r2 = r1 - x2.astype(F32)
    return x1, x2, r2.astype(BF16)


def _silu(x):
    return x * (1.0 / (1.0 + jnp.exp(-x)))


def _sigmoid(x):
    return 1.0 / (1.0 + jnp.exp(-x))


def _softplus(x):
    return jnp.maximum(x, 0.0) + jnp.log(1.0 + jnp.exp(-jnp.abs(x)))


def _rms(x, g):
    ms = jnp.mean(x * x, axis=-1, keepdims=True)
    return (x * lax.rsqrt(ms + EPS)) * g


def _block_diag(a, b):
    za, zb = jnp.zeros_like(a), jnp.zeros_like(b)
    return jnp.concatenate([jnp.concatenate([a, zb], axis=1), jnp.concatenate([za, b], axis=1)], axis=0)


def _inproj_kernel(has_res, *refs):
    if has_res:
        xa_ref, xb_ref, ln_ref, w_ref, wgt_ref, x_out, q_out, k_out, v_out, c_out, z_out, g_out, gt_out = refs
        x = xa_ref[...] + xb_ref[...]
        x_out[...] = x
    else:
        xa_ref, ln_ref, w_ref, wgt_ref, q_out, k_out, v_out, c_out, z_out, g_out, gt_out = refs
        x = xa_ref[...]
    h = _rms(x, ln_ref[...]).astype(BF16)
    q_out[...] = jnp.dot(h, w_ref[:, 0:512], preferred_element_type=F32).astype(BF16)
    k_out[...] = jnp.dot(h, w_ref[:, 512:1024], preferred_element_type=F32).astype(BF16)
    v_out[...] = jnp.dot(h, w_ref[:, 1024:1536], preferred_element_type=F32).astype(BF16)
    c_out[...] = jnp.dot(h, w_ref[:, 1536:3072], preferred_element_type=F32)
    z_out[...] = jnp.dot(h, w_ref[:, 3072:3584], preferred_element_type=F32)
    g_out[...] = jnp.dot(h, w_ref[:, GATE_COL0:IN_WIDTH_PAD], preferred_element_type=F32)
    gt_out[...] = lax.dot_general(wgt_ref[...], h, (((1,), (1,)), ((), ())), preferred_element_type=F32)


def _inproj(xa, xb, ln, w_pad, wgt):
    n = xa.shape[0]
    tm = ROW_TILE
    has_res = xb is not None
    row = lambda w: pl.BlockSpec((tm, w), lambda i: (i, 0))
    const = lambda s: pl.BlockSpec(s, lambda i: (0, 0))
    in_specs = [row(D_MODEL)] + ([row(D_MODEL)] if has_res else []) + [
        const((1, D_MODEL)), const((D_MODEL, IN_WIDTH_PAD)), const((16, D_MODEL))]
    out_shape = ([jax.ShapeDtypeStruct((n, D_MODEL), F32)] if has_res else []) + [
        jax.ShapeDtypeStruct((n, NA_WIDTH), BF16), jax.ShapeDtypeStruct((n, NA_WIDTH), BF16),
        jax.ShapeDtypeStruct((n, NA_WIDTH), BF16), jax.ShapeDtypeStruct((n, 3 * GDN_WIDTH), F32),
        jax.ShapeDtypeStruct((n, GDN_WIDTH), F32), jax.ShapeDtypeStruct((n, LANES), F32),
        jax.ShapeDtypeStruct((16, n), F32)]
    out_specs = ([row(D_MODEL)] if has_res else []) + [
        row(NA_WIDTH), row(NA_WIDTH), row(NA_WIDTH), row(3 * GDN_WIDTH), row(GDN_WIDTH), row(LANES),
        pl.BlockSpec((16, tm), lambda i: (0, i))]
    args = (xa,) + ((xb,) if has_res else ()) + (ln, w_pad, wgt)
    outs = pl.pallas_call(
        functools.partial(_inproj_kernel, has_res), grid=(n // tm,), in_specs=in_specs,
        out_specs=out_specs, out_shape=out_shape, compiler_params=_cparams(("parallel",)),
        name="inproj_res" if has_res else "inproj")(*args)
    if has_res:
        return outs[0], outs[1:]
    return xa, outs


def _na_kernel(rows, q_ref, k_ref, v_ref, bias_ref, o_ref):
    i = pl.program_id(2)
    lane = lax.broadcasted_iota(jnp.int32, (GRID_W, LANES), 1)
    low = lane < NA_HEAD_DIM
    scale = NA_HEAD_DIM ** -0.5

    def scores(rr):
        r = i * NA_ROW_BLOCK + rr
        rs = jnp.clip(r - NA_ROWS // 2, 0, rows - NA_ROWS)
        d0 = rs - r + NA_ROWS - 1
        k0 = pl.multiple_of(rs * GRID_W, GRID_W)
        kw = k_ref[0, pl.ds(k0, NA_ROWS * GRID_W), :]
        q = q_ref[0, rr * GRID_W:(rr + 1) * GRID_W, :]
        zq = jnp.zeros_like(q)
        q2 = jnp.concatenate([jnp.where(low, q, zq), jnp.where(low, zq, q)], axis=0)
        s = _bdot_nt(q2, kw) * scale
        bias = jnp.concatenate([bias_ref[0, d0], bias_ref[1, d0]], axis=0)
        return s + bias, k0

    def finish(rr, s, k0):
        vw = v_ref[0, pl.ds(k0, NA_ROWS * GRID_W), :]
        m = jnp.max(s, axis=-1, keepdims=True)
        p = jnp.exp(s - m)
        inv = 1.0 / jnp.sum(p, axis=-1, keepdims=True)
        o2 = _bdot(p, vw) * inv
        o_ref[0, rr * GRID_W:(rr + 1) * GRID_W, :] = jnp.where(low, o2[:GRID_W], o2[GRID_W:]).astype(o_ref.dtype)

    pending = scores(0)
    for rr in range(NA_ROW_BLOCK):
        nxt = scores(rr + 1) if rr + 1 < NA_ROW_BLOCK else None
        finish(rr, *pending)
        pending = nxt


def _na_bias_table(rpb):
    col = jnp.arange(GRID_W)
    cs = jnp.clip(col - NA_COLS // 2, 0, GRID_W - NA_COLS)
    kcol = jnp.arange(GRID_W)
    valid = (kcol[None, :] >= cs[:, None]) & (kcol[None, :] < cs[:, None] + NA_COLS)
    cidx = jnp.clip(kcol[None, :] - col[:, None] + NA_COLS - 1, 0, 2 * NA_COLS - 2)
    band = jnp.where(valid[None, None], rpb[:, :, cidx], NEG_BIG)
    ridx = jnp.arange(NA_ROWS)[:, None] + jnp.arange(NA_ROWS)[None, :]
    tab = band[:, ridx]
    tab = jnp.transpose(tab, (0, 1, 3, 2, 4))
    return tab.reshape(NA_HEADS, NA_ROWS, GRID_W, NA_ROWS * GRID_W).astype(F32)


def _neighbourhood_attention(q, k, v, bias_tab, batch):
    n = q.shape[0]
    seq = n // batch
    rows = seq // GRID_W
    q3, k3, v3 = (t.reshape(batch, seq, NA_WIDTH) for t in (q, k, v))
    tq = NA_ROW_BLOCK * GRID_W
    out = pl.pallas_call(
        functools.partial(_na_kernel, rows),
        grid=(batch, NA_HEADS // 2, rows // NA_ROW_BLOCK),
        in_specs=[pl.BlockSpec((1, tq, LANES), lambda b, hp, i: (b, i, hp)),
                  pl.BlockSpec((1, seq, LANES), lambda b, hp, i: (b, 0, hp)),
                  pl.BlockSpec((1, seq, LANES), lambda b, hp, i: (b, 0, hp)),
                  pl.BlockSpec((2, NA_ROWS, GRID_W, NA_ROWS * GRID_W), lambda b, hp, i: (hp, 0, 0, 0))],
        out_specs=pl.BlockSpec((1, tq, LANES), lambda b, hp, i: (b, i, hp)),
        out_shape=jax.ShapeDtypeStruct((batch, seq, NA_WIDTH), BF16),
        compiler_params=_cparams(("parallel", "parallel", "arbitrary")),
        name="na_attention")(q3, k3, v3, bias_tab)
    return out.reshape(n, NA_WIDTH)


def _gdn_prep_kernel(nblk, c_ref, prev_ref, next_ref, w_ref, q_out, k_out, v_out):
    i = pl.program_id(1)
    tm = c_ref.shape[1]
    pad = CONV_K // 2
    prev = jnp.where(i > 0, prev_ref[0], 0.0)
    nxt = jnp.where(i < nblk - 1, next_ref[0], 0.0)
    xc = jnp.concatenate([prev, c_ref[0], nxt], axis=0)
    acc = xc[8 - pad:8 - pad + tm] * w_ref[0:1, :]
    for j in range(1, CONV_K):
        acc = acc + xc[8 - pad + j:8 - pad + j + tm] * w_ref[j:j + 1, :]
    c = _silu(acc)
    for hd in range(GDN_HEADS):
        sl = slice(hd * GDN_HEAD_DIM, (hd + 1) * GDN_HEAD_DIM)
        qh = c[:, sl]
        kh = c[:, GDN_WIDTH + hd * GDN_HEAD_DIM:GDN_WIDTH + (hd + 1) * GDN_HEAD_DIM]
        q_out[0, :, sl] = qh * lax.rsqrt(jnp.sum(qh * qh, axis=-1, keepdims=True) + EPS)
        k_out[0, :, sl] = kh * lax.rsqrt(jnp.sum(kh * kh, axis=-1, keepdims=True) + EPS)
    v_out[0] = c[:, 2 * GDN_WIDTH:]


def _gdn_prep(c, conv_w, batch):
    n = c.shape[0]
    seq = n // batch
    tm = ROW_TILE
    nblk = seq // tm
    c3 = c.reshape(batch, seq, 3 * GDN_WIDTH)
    hb = tm // 8
    spec = pl.BlockSpec((1, tm, GDN_WIDTH), lambda b, i: (b, i, 0))
    outs = pl.pallas_call(
        functools.partial(_gdn_prep_kernel, nblk),
        grid=(batch, nblk),
        in_specs=[pl.BlockSpec((1, tm, 3 * GDN_WIDTH), lambda b, i: (b, i, 0)),
                  pl.BlockSpec((1, 8, 3 * GDN_WIDTH), lambda b, i: (b, jnp.maximum(i * hb - 1, 0), 0)),
                  pl.BlockSpec((1, 8, 3 * GDN_WIDTH), lambda b, i: (b, jnp.minimum((i + 1) * hb, seq // 8 - 1), 0)),
                  pl.BlockSpec((8, 3 * GDN_WIDTH), lambda b, i: (0, 0))],
        out_specs=[spec, spec, spec],
        out_shape=[jax.ShapeDtypeStruct((batch, seq, GDN_WIDTH), F32)] * 3,
        compiler_params=_cparams(("parallel", "parallel")),
        name="gdn_prep")(c3, c3, c3, conv_w)
    return tuple(t.reshape(n, GDN_WIDTH) for t in outs)


def _gdn_scan_kernel(qf, kf, vf, gf, gtf, qb, kb, vb, gb, gtb, arow, drow, acol, dcol, of_ref, ob_ref, s_ref):
    @pl.when(pl.program_id(1) == 0)
    def _():
        s_ref[...] = jnp.zeros_like(s_ref)

    cb = GDN_BLOCK_CHUNKS
    hw = GDN_HEAD_DIM
    npair = GDN_HEADS // 2
    ri = lax.broadcasted_iota(jnp.int32, (CHUNK, LANES), 0)
    lane = lax.broadcasted_iota(jnp.int32, (CHUNK, LANES), 1)
    lo = lane < CHUNK
    cm = jnp.bitwise_and(lane, CHUNK - 1)
    eye = jnp.where(ri == cm, 1.0, 0.0)
    rr = lax.broadcasted_iota(jnp.int32, (CHUNK, CHUNK), 0)
    cc = lax.broadcasted_iota(jnp.int32, (CHUNK, CHUNK), 1)
    neg_a_row, neg_a_col = -jnp.exp(arow[...]), -jnp.exp(acol[...])
    d_row, d_col = drow[...], dcol[...]
    refs = ((qf, kf, vf, gf, gtf, of_ref), (qb, kb, vb, gb, gtb, ob_ref))
    incl = (ri >= cm, ri <= cm)
    strict = (ri > cm, ri < cm)
    tri = (jnp.where(rr >= cc, 1.0, 0.0).astype(BF16), jnp.where(rr <= cc, 1.0, 0.0).astype(BF16))
    tri_t2 = (jnp.where(cm >= ri, 1.0, 0.0).astype(BF16), jnp.where(cm <= ri, 1.0, 0.0).astype(BF16))
    f32dot = lambda a, b: jnp.dot(a, b, preferred_element_type=F32)

    inst = [(d, c) for d in (0, 1) for c in range(cb)]
    pairs = [(d, c, p) for d, c in inst for p in range(npair)]

    st = {}
    for d, c in inst:
        q_ref, k_ref, v_ref, g_ref, gt_ref, _ = refs[d]
        sl = slice(c * CHUNK, (c + 1) * CHUNK)
        graw = g_ref[0, sl, :]
        g1, g2, g3 = _split3(neg_a_row * _softplus(graw + d_row))
        t1, t2, t3 = _split3(neg_a_col * _softplus(gt_ref[0, c] + d_col))
        st[d, c] = dict(
            gc_full=f32dot(tri[d], g1) + f32dot(tri[d], g2) + f32dot(tri[d], g3),
            gct2=f32dot(t1, tri_t2[d]) + f32dot(t2, tri_t2[d]) + f32dot(t3, tri_t2[d]),
            beta_full=_sigmoid(graw),
            q=q_ref[0, sl, :], k=k_ref[0, sl, :], v=v_ref[0, sl, :])

    pk = {}
    for d, c, p in pairs:
        s = st[d, c]
        h0, h1 = 2 * p, 2 * p + 1
        col = lambda full, base: jnp.where(lo, full[:, base + h0:base + h0 + 1], full[:, base + h1:base + h1 + 1])
        gc = col(s["gc_full"], 8 + 4 * d)
        beta = col(s["beta_full"], 4 * d)
        g_row0 = 8 + 4 * d + h0
        gr = jnp.where(lo[0:1], s["gct2"][g_row0:g_row0 + 1, :], s["gct2"][g_row0 + 1:g_row0 + 2, :])
        hs = lambda t, h: t[:, h * hw:(h + 1) * hw]
        q0, q1 = hs(s["q"], h0) * (hw ** -0.5), hs(s["q"], h1) * (hw ** -0.5)
        k0, k1 = hs(s["k"], h0), hs(s["k"], h1)
        lhs = jnp.concatenate([jnp.concatenate([q0, k0], axis=0), jnp.concatenate([q1, k1], axis=0)], axis=1)
        qkk = _bdot_nt(lhs, _block_diag(k0.astype(BF16), k1.astype(BF16)))
        decay = jnp.where(incl[d], jnp.exp(jnp.where(incl[d], gc - gr, 0.0)), 0.0)
        last = CHUNK - 1 if d == 0 else 0
        g_last = gc[last:last + 1, :]
        pk[d, c, p] = dict(qk=qkk[:CHUNK] * decay, pw=jnp.where(strict[d], qkk[CHUNK:] * beta * decay, 0.0),
                           beta=beta, egc=jnp.exp(gc), g_last=g_last, dk=jnp.exp(g_last - gc),
                           q=(q0, q1), k=(k0, k1), v=(hs(s["v"], h0), hs(s["v"], h1)))

    def bd_of(pw):
        z = jnp.zeros_like(pw)
        return jnp.concatenate([jnp.where(lo, pw, z), jnp.where(lo, z, pw)], axis=0)

    for key in pairs:
        e = pk[key]
        e["tinv"] = eye - e["pw"]
        pw = e["pw"].astype(BF16)
        e["pw"] = f32dot(pw, bd_of(pw))
    for level in range(1, 6):
        for key in pairs:
            e = pk[key]
            pw = e["pw"].astype(BF16)
            if level < 5:
                r = f32dot(jnp.concatenate([pw, e["tinv"].astype(BF16)], axis=0), bd_of(pw))
                e["pw"] = r[:CHUNK]
                e["tinv"] = e["tinv"] + r[CHUNK:]
            else:
                e["tinv"] = e["tinv"] + f32dot(e["tinv"].astype(BF16), bd_of(pw))

    for key in pairs:
        e = pk[key]
        halves = lambda t: (t[:, 0:1], t[:, LANES - 1:LANES])
        b, g, dk = halves(e["beta"]), halves(e["egc"]), halves(e["dk"])
        rhs = [jnp.concatenate([e["v"][hh] * b[hh], e["k"][hh] * (b[hh] * g[hh])], axis=1).astype(BF16)
               for hh in range(2)]
        uw = f32dot(e["tinv"].astype(BF16), _block_diag(rhs[0], rhs[1]))
        e["u"] = (uw[:, 0:hw], uw[:, 2 * hw:3 * hw])
        e["w"] = (uw[:, hw:2 * hw], uw[:, 3 * hw:4 * hw])
        e["qg"] = (e["q"][0] * g[0], e["q"][1] * g[1])
        e["kg"] = (e["k"][0] * dk[0], e["k"][1] * dk[1])
        e["sdec"] = (jnp.exp(e["g_last"][:, 0:1]), jnp.exp(e["g_last"][:, LANES - 1:LANES]))

    for j in range(cb):
        step = [(0, j, p) for p in range(npair)] + [(1, cb - 1 - j, p) for p in range(npair)]
        s_old, wsqs, v_new = {}, {}, {}
        for d, c, p in step:
            e = pk[d, c, p]
            for hh in range(2):
                s_old[d, p, hh] = s_ref[d, 2 * p + hh]
                wsqs[d, p, hh] = _bdot(jnp.concatenate([e["w"][hh], e["qg"][hh]], axis=0), s_old[d, p, hh])
        for d, c, p in step:
            for hh in range(2):
                v_new[d, p, hh] = pk[d, c, p]["u"][hh] - wsqs[d, p, hh][:CHUNK]
        for d, c, p in step:
            e = pk[d, c, p]
            bdv = _block_diag(v_new[d, p, 0].astype(BF16), v_new[d, p, 1].astype(BF16))
            o_pair = f32dot(e["qk"].astype(BF16), bdv)
            o_pair = o_pair + jnp.concatenate([wsqs[d, p, 0][CHUNK:], wsqs[d, p, 1][CHUNK:]], axis=1)
            refs[d][5][0, c * CHUNK:(c + 1) * CHUNK, 2 * p * hw:(2 * p + 2) * hw] = o_pair
            for hh in range(2):
                s_ref[d, 2 * p + hh] = s_old[d, p, hh] * e["sdec"][hh] + _bdot_tn(e["kg"][hh], v_new[d, p, hh])


def _gdn_scan(q, k, v, gate, gate_t, a_log, dt_bias, batch):
    n = q.shape[0]
    seq = n // batch
    tb = GDN_BLOCK_CHUNKS * CHUNK
    nblk = seq // tb
    q3, k3, v3 = (t.reshape(batch, seq, GDN_WIDTH) for t in (q, k, v))
    g3 = gate.reshape(batch, seq, LANES)
    gt4 = jnp.transpose(gate_t.reshape(16, batch, seq // CHUNK, CHUNK), (1, 2, 0, 3))
    arow = jnp.zeros((1, LANES), F32).at[0, 8:16].set(a_log.reshape(-1))
    drow = jnp.zeros((1, LANES), F32).at[0, 8:16].set(dt_bias.reshape(-1))
    acol = arow[0, :16].reshape(16, 1)
    dcol = drow[0, :16].reshape(16, 1)
    fwd = lambda w: pl.BlockSpec((1, tb, w), lambda b, i: (b, i, 0))
    bwd = lambda w: pl.BlockSpec((1, tb, w), lambda b, i: (b, nblk - 1 - i, 0))
    gtf = pl.BlockSpec((1, GDN_BLOCK_CHUNKS, 16, CHUNK), lambda b, i: (b, i, 0, 0))
    gtb = pl.BlockSpec((1, GDN_BLOCK_CHUNKS, 16, CHUNK), lambda b, i: (b, nblk - 1 - i, 0, 0))
    const = lambda s: pl.BlockSpec(s, lambda b, i: (0, 0))
    o_f, o_b = pl.pallas_call(
        _gdn_scan_kernel, grid=(batch, nblk),
        in_specs=[fwd(GDN_WIDTH), fwd(GDN_WIDTH), fwd(GDN_WIDTH), fwd(LANES), gtf,
                  bwd(GDN_WIDTH), bwd(GDN_WIDTH), bwd(GDN_WIDTH), bwd(LANES), gtb,
                  const((1, LANES)), const((1, LANES)), const((16, 1)), const((16, 1))],
        out_specs=[fwd(GDN_WIDTH), bwd(GDN_WIDTH)],
        out_shape=[jax.ShapeDtypeStruct((batch, seq, GDN_WIDTH), F32)] * 2,
        scratch_shapes=[pltpu.VMEM((2, GDN_HEADS, GDN_HEAD_DIM, GDN_HEAD_DIM), F32)],
        compiler_params=_cparams(("parallel", "arbitrary")),
        name="gdn_scan")(q3, k3, v3, g3, gt4, q3, k3, v3, g3, gt4, arow, drow, acol, dcol)
    return o_f.reshape(n, GDN_WIDTH), o_b.reshape(n, GDN_WIDTH)


def _outproj_kernel(x_ref, na_ref, of_ref, ob_ref, z_ref, gn_ref, w_ref, ln_ref, wr_ref, x1_out, h_out, aff_out):
    o = of_ref[...] + ob_ref[...]
    z = z_ref[...]
    gn = gn_ref[...]
    acc = jnp.dot(na_ref[...], w_ref[0:NA_WIDTH, :], preferred_element_type=F32)
    for hd in range(GDN_HEADS):
        sl = slice(hd * GDN_HEAD_DIM, (hd + 1) * GDN_HEAD_DIM)
        oh = o[:, sl]
        oh = oh * lax.rsqrt(jnp.mean(oh * oh, axis=-1, keepdims=True) + EPS) * gn
        oh = oh * _silu(z[:, sl])
        acc = acc + jnp.dot(oh.astype(BF16), w_ref[NA_WIDTH + hd * GDN_HEAD_DIM:NA_WIDTH + (hd + 1) * GDN_HEAD_DIM, :],
                            preferred_element_type=F32)
    x1 = x_ref[...] + acc
    x1_out[...] = x1
    h = _rms(x1, ln_ref[...])
    h_out[...] = h.astype(BF16)
    h1, h2, h3 = _split3(h)
    w1, w2, w3 = _split3(wr_ref[...])
    nt = lambda a, b: lax.dot_general(a, b, (((1,), (1,)), ((), ())), preferred_element_type=F32)
    logits = (nt(w1, h1) + (nt(w1, h2) + nt(w2, h1)) + (nt(w1, h3) + nt(w2, h2) + nt(w3, h1)))
    m = jnp.max(logits, axis=0, keepdims=True)
    e = jnp.exp(logits - m)
    aff_out[...] = e / jnp.sum(e, axis=0, keepdims=True)


def _outproj(x, na, o_f, o_b, z, gdn_norm, w_out, ln2, w_router_t):
    n = x.shape[0]
    tm = ROW_TILE
    row = lambda w: pl.BlockSpec((tm, w), lambda i: (i, 0))
    const = lambda s: pl.BlockSpec(s, lambda i: (0, 0))
    return pl.pallas_call(
        _outproj_kernel, grid=(n // tm,),
        in_specs=[row(D_MODEL), row(NA_WIDTH), row(GDN_WIDTH), row(GDN_WIDTH), row(GDN_WIDTH),
                  const((1, GDN_HEAD_DIM)), const((D_MODEL, D_MODEL)), const((1, D_MODEL)), const((N_EXPERTS, D_MODEL))],
        out_specs=[row(D_MODEL), row(D_MODEL), pl.BlockSpec((N_EXPERTS, tm), lambda i: (0, i))],
        out_shape=[jax.ShapeDtypeStruct((n, D_MODEL), F32), jax.ShapeDtypeStruct((n, D_MODEL), BF16),
                   jax.ShapeDtypeStruct((N_EXPERTS, n), F32)],
        compiler_params=_cparams(("parallel",)),
        name="outproj_router")(x, na, o_f, o_b, z, gdn_norm, w_out, ln2, w_router_t)


def _ffn_kernel(x_ref, g_ref, wg_ref, wu_ref, wd_ref, y_out):
    x = x_ref[0]
    a = jnp.dot(x, wg_ref[0], preferred_element_type=F32)
    b = jnp.dot(x, wu_ref[0], preferred_element_type=F32)
    hid = (_silu(a) * b).astype(BF16)
    y_out[0] = jnp.dot(hid, wd_ref[0], preferred_element_type=F32) * g_ref[0]


def _expert_ffn(xe, gates, w_gate, w_up, w_down):
    e, r, _ = xe.shape
    tm = FFN_ROW_TILE
    wspec = pl.BlockSpec((1, D_MODEL, EXPERT_FF), lambda ei, j: (ei, 0, 0))
    return pl.pallas_call(
        _ffn_kernel, grid=(e, r // tm),
        in_specs=[pl.BlockSpec((1, tm, D_MODEL), lambda ei, j: (ei, j, 0)),
                  pl.BlockSpec((1, tm, 1), lambda ei, j: (ei, j, 0)),
                  wspec, wspec, pl.BlockSpec((1, EXPERT_FF, D_MODEL), lambda ei, j: (ei, 0, 0))],
        out_specs=pl.BlockSpec((1, tm, D_MODEL), lambda ei, j: (ei, j, 0)),
        out_shape=jax.ShapeDtypeStruct((e, r, D_MODEL), F32),
        compiler_params=_cparams(("parallel", "arbitrary")),
        name="expert_ffn")(xe, gates, w_gate, w_up, w_down)


def _final_kernel(xa_ref, xb_ref, ln_ref, o_ref):
    o_ref[...] = _rms(xa_ref[...] + xb_ref[...], ln_ref[...])


def _final_norm(xa, xb, ln):
    n = xa.shape[0]
    tm = ROW_TILE
    row = pl.BlockSpec((tm, D_MODEL), lambda i: (i, 0))
    return pl.pallas_call(
        _final_kernel, grid=(n // tm,), in_specs=[row, row, pl.BlockSpec((1, D_MODEL), lambda i: (0, 0))],
        out_specs=row, out_shape=jax.ShapeDtypeStruct((n, D_MODEL), F32),
        compiler_params=_cparams(("parallel",)), name="final_norm")(xa, xb, ln)


def _moe(h_bf, aff_t, wg, wu, wd):
    n = h_bf.shape[0]
    cap = EC_CAPACITY * n // N_EXPERTS
    gates, idx = lax.top_k(aff_t, cap)
    xe = jnp.take(h_bf, idx, axis=0)
    ye = _expert_ffn(xe, gates[..., None], wg, wu, wd)
    return jnp.zeros((n, D_MODEL), F32).at[idx.reshape(-1)].add(ye.reshape(-1, D_MODEL))


def _prep_layer(l, ln1, w_in, conv_w, a_log, dt_bias, gdn_norm, rpb, w_out, ln2, w_router, w_gate, w_up, w_down):
    w_pad = jnp.pad(w_in[l], ((0, 0), (0, IN_WIDTH_PAD - IN_WIDTH))).astype(BF16)
    return dict(
        ln1=ln1[l].reshape(1, D_MODEL), w_pad=w_pad,
        wgt=jnp.transpose(w_in[l][:, GATE_COL0:IN_WIDTH]).astype(BF16),
        conv_w=jnp.pad(conv_w[l], ((0, 8 - CONV_K), (0, 0))),
        a_log=a_log[l], dt_bias=dt_bias[l], gdn_norm=gdn_norm[l].reshape(1, GDN_HEAD_DIM),
        bias_tab=_na_bias_table(rpb[l]), w_out=w_out[l].astype(BF16), ln2=ln2[l].reshape(1, D_MODEL),
        w_router_t=jnp.transpose(w_router[l]),
        wg=w_gate[l].astype(BF16), wu=w_up[l].astype(BF16), wd=w_down[l].astype(BF16))


def _trunk(x3, layers, ln_f):
    batch, seq, _ = x3.shape
    xa = x3.reshape(batch * seq, D_MODEL)
    xb = None
    for p in layers:
        x, (q, k, v, c, z, gate, gate_t) = _inproj(xa, xb, p["ln1"], p["w_pad"], p["wgt"])
        na = _neighbourhood_attention(q, k, v, p["bias_tab"], batch)
        gq, gk, gv = _gdn_prep(c, p["conv_w"], batch)
        o_f, o_b = _gdn_scan(gq, gk, gv, gate, gate_t, p["a_log"], p["dt_bias"], batch)
        x1, h_bf, aff_t = _outproj(x, na, o_f, o_b, z, p["gdn_norm"], p["w_out"], p["ln2"], p["w_router_t"])
        xa, xb = x1, _moe(h_bf, aff_t, p["wg"], p["wu"], p["wd"])
    return _final_norm(xa, xb, ln_f.reshape(1, D_MODEL)).reshape(batch, seq, D_MODEL)


def kernel(x_prompt, x_sample, ln1, w_in, conv_w, a_log, dt_bias, gdn_norm, rpb, w_out, ln2, w_router, w_gate, w_up, w_down, ln_f):
    layers = [_prep_layer(l, ln1, w_in, conv_w, a_log, dt_bias, gdn_norm, rpb, w_out, ln2, w_router,
                          w_gate, w_up, w_down) for l in range(ln1.shape[0])]
    return (_trunk(x_prompt, layers, ln_f), _trunk(x_sample, layers, ln_f))
```

```python
import functools

import jax
import jax.numpy as jnp
from jax import lax
from jax.experimental import pallas as pl
from jax.experimental.pallas import tpu as pltpu

F32 = jnp.float32
BF16 = jnp.bfloat16

D_MODEL = 1024
GRID_W = 64
NA_HEADS = 8
NA_HEAD_DIM = 64
NA_WIDTH = 512
NA_ROWS = 8
NA_COLS = 16
GDN_HEADS = 4
GDN_HEAD_DIM = 128
GDN_WIDTH = 512
CONV_K = 5
CHUNK = 64
IN_WIDTH = 3600
N_EXPERTS = 16
EXPERT_FF = 1024
EC_CAPACITY = 2
EPS = 1e-6

LANES = 128
IN_WIDTH_PAD = 3712
GATE_COL0 = 3584
NEG_BIG = -1e30
VMEM_LIMIT = 56 * 1024 * 1024

ROW_TILE = 512
NA_ROW_BLOCK = 8
GDN_BLOCK_CHUNKS = 4
FFN_ROW_TILE = 512
MOE_TOKEN_TILE = 256
MOE_SLOTS = 64
MOE_SLOT_SHIFT = 4
MOE_SLOT_ALIGN = 1 << MOE_SLOT_SHIFT


def _cparams(sem):
    return pltpu.CompilerParams(dimension_semantics=sem, vmem_limit_bytes=VMEM_LIMIT)


def _bdot(a, b):
    return jnp.dot(a.astype(BF16), b.astype(BF16), preferred_element_type=F32)


def _bdot_nt(a, b):
    return lax.dot_general(a.astype(BF16), b.astype(BF16), (((1,), (1,)), ((), ())),
                           preferred_element_type=F32)


def _bdot_tn(a, b):
    return lax.dot_general(a.astype(BF16), b.astype(BF16), (((0,), (0,)), ((), ())),
                           preferred_element_type=F32)


def _split3(x):
    x1 = x.astype(BF16)
    r1 = x - x1.astype(F32)
    x2 = r1.astype(BF16)
    r2 = r1 - x2.astype(F32)
    return x1, x2, r2.astype(BF16)


def _silu(x):
    return x * (1.0 / (1.0 + jnp.exp(-x)))


def _sigmoid(x):
    return 1.0 / (1.0 + jnp.exp(-x))


def _softplus(x):
    return jnp.maximum(x, 0.0) + jnp.log(1.0 + jnp.exp(-jnp.abs(x)))


def _rms(x, g):
    ms = jnp.mean(x * x, axis=-1, keepdims=True)
    return (x * lax.rsqrt(ms + EPS)) * g


def _block_diag(a, b):
    za, zb = jnp.zeros_like(a), jnp.zeros_like(b)
    return jnp.concatenate([jnp.concatenate([a, zb], axis=1), jnp.concatenate([za, b], axis=1)], axis=0)


def _inproj_kernel(x_ref, ln_ref, w_ref, wgt_ref, q_out, k_out, v_out, c_out, z_out, g_out, gt_out):
    h = _rms(x_ref[...], ln_ref[...]).astype(BF16)
    q_out[...] = jnp.dot(h, w_ref[:, 0:512], preferred_element_type=F32).astype(BF16)
    k_out[...] = jnp.dot(h, w_ref[:, 512:1024], preferred_element_type=F32).astype(BF16)
    v_out[...] = jnp.dot(h, w_ref[:, 1024:1536], preferred_element_type=F32).astype(BF16)
    c_out[...] = jnp.dot(h, w_ref[:, 1536:3072], preferred_element_type=F32)
    z_out[...] = jnp.dot(h, w_ref[:, 3072:3584], preferred_element_type=F32)
    g_out[...] = jnp.dot(h, w_ref[:, GATE_COL0:IN_WIDTH_PAD], preferred_element_type=F32)
    gt_out[...] = lax.dot_general(wgt_ref[...], h, (((1,), (1,)), ((), ())), preferred_element_type=F32)


def _inproj(x, ln, w_pad, wgt):
    n = x.shape[0]
    tm = ROW_TILE
    row = lambda w: pl.BlockSpec((tm, w), lambda i: (i, 0))
    const = lambda s: pl.BlockSpec(s, lambda i: (0, 0))
    out_shape = [
        jax.ShapeDtypeStruct((n, NA_WIDTH), BF16), jax.ShapeDtypeStruct((n, NA_WIDTH), BF16),
        jax.ShapeDtypeStruct((n, NA_WIDTH), BF16), jax.ShapeDtypeStruct((n, 3 * GDN_WIDTH), F32),
        jax.ShapeDtypeStruct((n, GDN_WIDTH), F32), jax.ShapeDtypeStruct((n, LANES), F32),
        jax.ShapeDtypeStruct((16, n), F32)]
    out_specs = [row(NA_WIDTH), row(NA_WIDTH), row(NA_WIDTH), row(3 * GDN_WIDTH), row(GDN_WIDTH), row(LANES),
                 pl.BlockSpec((16, tm), lambda i: (0, i))]
    return pl.pallas_call(
        _inproj_kernel, grid=(n // tm,),
        in_specs=[row(D_MODEL), const((1, D_MODEL)), const((D_MODEL, IN_WIDTH_PAD)), const((16, D_MODEL))],
        out_specs=out_specs, out_shape=out_shape, compiler_params=_cparams(("parallel",)),
        name="inproj")(x, ln, w_pad, wgt)


def _na_kernel(rows, q_ref, k_ref, v_ref, bias_ref, o_ref):
    i = pl.program_id(2)
    lane = lax.broadcasted_iota(jnp.int32, (GRID_W, LANES), 1)
    low = lane < NA_HEAD_DIM
    scale = NA_HEAD_DIM ** -0.5

    def scores(rr):
        r = i * NA_ROW_BLOCK + rr
        rs = jnp.clip(r - NA_ROWS // 2, 0, rows - NA_ROWS)
        d0 = rs - r + NA_ROWS - 1
        k0 = pl.multiple_of(rs * GRID_W, GRID_W)
        kw = k_ref[0, pl.ds(k0, NA_ROWS * GRID_W), :]
        q = q_ref[0, rr * GRID_W:(rr + 1) * GRID_W, :]
        zq = jnp.zeros_like(q)
        q2 = jnp.concatenate([jnp.where(low, q, zq), jnp.where(low, zq, q)], axis=0)
        s = _bdot_nt(q2, kw) * scale
        bias = jnp.concatenate([bias_ref[0, d0], bias_ref[1, d0]], axis=0)
        return s + bias, k0

    def finish(rr, s, k0):
        vw = v_ref[0, pl.ds(k0, NA_ROWS * GRID_W), :]
        m = jnp.max(s, axis=-1, keepdims=True)
        p = jnp.exp(s - m)
        inv = 1.0 / jnp.sum(p, axis=-1, keepdims=True)
        o2 = _bdot(p, vw) * inv
        o_ref[0, rr * GRID_W:(rr + 1) * GRID_W, :] = jnp.where(low, o2[:GRID_W], o2[GRID_W:]).astype(o_ref.dtype)

    pending = scores(0)
    for rr in range(NA_ROW_BLOCK):
        nxt = scores(rr + 1) if rr + 1 < NA_ROW_BLOCK else None
        finish(rr, *pending)
        pending = nxt


def _na_bias_table(rpb):
    col = jnp.arange(GRID_W)
    cs = jnp.clip(col - NA_COLS // 2, 0, GRID_W - NA_COLS)
    kcol = jnp.arange(GRID_W)
    valid = (kcol[None, :] >= cs[:, None]) & (kcol[None, :] < cs[:, None] + NA_COLS)
    cidx = jnp.clip(kcol[None, :] - col[:, None] + NA_COLS - 1, 0, 2 * NA_COLS - 2)
    band = jnp.where(valid[None, None], rpb[:, :, cidx], NEG_BIG)
    ridx = jnp.arange(NA_ROWS)[:, None] + jnp.arange(NA_ROWS)[None, :]
    tab = band[:, ridx]
    tab = jnp.transpose(tab, (0, 1, 3, 2, 4))
    return tab.reshape(NA_HEADS, NA_ROWS, GRID_W, NA_ROWS * GRID_W).astype(F32)


def _neighbourhood_attention(q, k, v, bias_tab, batch):
    n = q.shape[0]
    seq = n // batch
    rows = seq // GRID_W
    q3, k3, v3 = (t.reshape(batch, seq, NA_WIDTH) for t in (q, k, v))
    tq = NA_ROW_BLOCK * GRID_W
    out = pl.pallas_call(
        functools.partial(_na_kernel, rows),
        grid=(batch, NA_HEADS // 2, rows // NA_ROW_BLOCK),
        in_specs=[pl.BlockSpec((1, tq, LANES), lambda b, hp, i: (b, i, hp)),
                  pl.BlockSpec((1, seq, LANES), lambda b, hp, i: (b, 0, hp)),
                  pl.BlockSpec((1, seq, LANES), lambda b, hp, i: (b, 0, hp)),
                  pl.BlockSpec((2, NA_ROWS, GRID_W, NA_ROWS * GRID_W), lambda b, hp, i: (hp, 0, 0, 0))],
        out_specs=pl.BlockSpec((1, tq, LANES), lambda b, hp, i: (b, i, hp)),
        out_shape=jax.ShapeDtypeStruct((batch, seq, NA_WIDTH), BF16),
        compiler_params=_cparams(("parallel", "parallel", "arbitrary")),
        name="na_attention")(q3, k3, v3, bias_tab)
    return out.reshape(n, NA_WIDTH)


def _gdn_prep_kernel(nblk, c_ref, prev_ref, next_ref, w_ref, q_out, k_out, v_out):
    i = pl.program_id(1)
    tm = c_ref.shape[1]
    pad = CONV_K // 2
    prev = jnp.where(i > 0, prev_ref[0], 0.0)
    nxt = jnp.where(i < nblk - 1, next_ref[0], 0.0)
    xc = jnp.concatenate([prev, c_ref[0], nxt], axis=0)
    acc = xc[8 - pad:8 - pad + tm] * w_ref[0:1, :]
    for j in range(1, CONV_K):
        acc = acc + xc[8 - pad + j:8 - pad + j + tm] * w_ref[j:j + 1, :]
    c = _silu(acc)
    for hd in range(GDN_HEADS):
        sl = slice(hd * GDN_HEAD_DIM, (hd + 1) * GDN_HEAD_DIM)
        qh = c[:, sl]
        kh = c[:, GDN_WIDTH + hd * GDN_HEAD_DIM:GDN_WIDTH + (hd + 1) * GDN_HEAD_DIM]
        q_out[0, :, sl] = qh * lax.rsqrt(jnp.sum(qh * qh, axis=-1, keepdims=True) + EPS)
        k_out[0, :, sl] = kh * lax.rsqrt(jnp.sum(kh * kh, axis=-1, keepdims=True) + EPS)
    v_out[0] = c[:, 2 * GDN_WIDTH:]


def _gdn_prep(c, conv_w, batch):
    n = c.shape[0]
    seq = n // batch
    tm = ROW_TILE
    nblk = seq // tm
    c3 = c.reshape(batch, seq, 3 * GDN_WIDTH)
    hb = tm // 8
    spec = pl.BlockSpec((1, tm, GDN_WIDTH), lambda b, i: (b, i, 0))
    outs = pl.pallas_call(
        functools.partial(_gdn_prep_kernel, nblk),
        grid=(batch, nblk),
        in_specs=[pl.BlockSpec((1, tm, 3 * GDN_WIDTH), lambda b, i: (b, i, 0)),
                  pl.BlockSpec((1, 8, 3 * GDN_WIDTH), lambda b, i: (b, jnp.maximum(i * hb - 1, 0), 0)),
                  pl.BlockSpec((1, 8, 3 * GDN_WIDTH), lambda b, i: (b, jnp.minimum((i + 1) * hb, seq // 8 - 1), 0)),
                  pl.BlockSpec((8, 3 * GDN_WIDTH), lambda b, i: (0, 0))],
        out_specs=[spec, spec, spec],
        out_shape=[jax.ShapeDtypeStruct((batch, seq, GDN_WIDTH), F32)] * 3,
        compiler_params=_cparams(("parallel", "parallel")),
        name="gdn_prep")(c3, c3, c3, conv_w)
    return tuple(t.reshape(n, GDN_WIDTH) for t in outs)


def _gdn_scan_kernel(qf, kf, vf, gf, gtf, qb, kb, vb, gb, gtb, arow, drow, acol, dcol, of_ref, ob_ref, s_ref):
    @pl.when(pl.program_id(1) == 0)
    def _():
        s_ref[...] = jnp.zeros_like(s_ref)

    cb = GDN_BLOCK_CHUNKS
    hw = GDN_HEAD_DIM
    npair = GDN_HEADS // 2
    ri = lax.broadcasted_iota(jnp.int32, (CHUNK, LANES), 0)
    lane = lax.broadcasted_iota(jnp.int32, (CHUNK, LANES), 1)
    lo = lane < CHUNK
    cm = jnp.bitwise_and(lane, CHUNK - 1)
    eye = jnp.where(ri == cm, 1.0, 0.0)
    rr = lax.broadcasted_iota(jnp.int32, (CHUNK, CHUNK), 0)
    cc = lax.broadcasted_iota(jnp.int32, (CHUNK, CHUNK), 1)
    neg_a_row, neg_a_col = -jnp.exp(arow[...]), -jnp.exp(acol[...])
    d_row, d_col = drow[...], dcol[...]
    refs = ((qf, kf, vf, gf, gtf, of_ref), (qb, kb, vb, gb, gtb, ob_ref))
    incl = (ri >= cm, ri <= cm)
    strict = (ri > cm, ri < cm)
    tri = (jnp.where(rr >= cc, 1.0, 0.0).astype(BF16), jnp.where(rr <= cc, 1.0, 0.0).astype(BF16))
    tri_t2 = (jnp.where(cm >= ri, 1.0, 0.0).astype(BF16), jnp.where(cm <= ri, 1.0, 0.0).astype(BF16))
    f32dot = lambda a, b: jnp.dot(a, b, preferred_element_type=F32)

    inst = [(d, c) for d in (0, 1) for c in range(cb)]
    pairs = [(d, c, p) for d, c in inst for p in range(npair)]

    st = {}
    for d, c in inst:
        q_ref, k_ref, v_ref, g_ref, gt_ref, _ = refs[d]
        sl = slice(c * CHUNK, (c + 1) * CHUNK)
        graw = g_ref[0, sl, :]
        g1, g2, g3 = _split3(neg_a_row * _softplus(graw + d_row))
        t1, t2, t3 = _split3(neg_a_col * _softplus(gt_ref[0, c] + d_col))
        st[d, c] = dict(
            gc_full=f32dot(tri[d], g1) + f32dot(tri[d], g2) + f32dot(tri[d], g3),
            gct2=f32dot(t1, tri_t2[d]) + f32dot(t2, tri_t2[d]) + f32dot(t3, tri_t2[d]),
            beta_full=_sigmoid(graw),
            q=q_ref[0, sl, :], k=k_ref[0, sl, :], v=v_ref[0, sl, :])

    pk = {}
    for d, c, p in pairs:
        s = st[d, c]
        h0, h1 = 2 * p, 2 * p + 1
        col = lambda full, base: jnp.where(lo, full[:, base + h0:base + h0 + 1], full[:, base + h1:base + h1 + 1])
        gc = col(s["gc_full"], 8 + 4 * d)
        beta = col(s["beta_full"], 4 * d)
        g_row0 = 8 + 4 * d + h0
        gr = jnp.where(lo[0:1], s["gct2"][g_row0:g_row0 + 1, :], s["gct2"][g_row0 + 1:g_row0 + 2, :])
        hs = lambda t, h: t[:, h * hw:(h + 1) * hw]
        q0, q1 = hs(s["q"], h0) * (hw ** -0.5), hs(s["q"], h1) * (hw ** -0.5)
        k0, k1 = hs(s["k"], h0), hs(s["k"], h1)
        lhs = jnp.concatenate([jnp.concatenate([q0, k0], axis=0), jnp.concatenate([q1, k1], axis=0)], axis=1)
        qkk = _bdot_nt(lhs, _block_diag(k0.astype(BF16), k1.astype(BF16)))
        decay = jnp.where(incl[d], jnp.exp(jnp.where(incl[d], gc - gr, 0.0)), 0.0)
        last = CHUNK - 1 if d == 0 else 0
        g_last = gc[last:last + 1, :]
        pk[d, c, p] = dict(qk=qkk[:CHUNK] * decay, pw=jnp.where(strict[d], qkk[CHUNK:] * beta * decay, 0.0),
                           beta=beta, egc=jnp.exp(gc), g_last=g_last, dk=jnp.exp(g_last - gc),
                           q=(q0, q1), k=(k0, k1), v=(hs(s["v"], h0), hs(s["v"], h1)))

    def bd_of(pw):
        z = jnp.zeros_like(pw)
        return jnp.concatenate([jnp.where(lo, pw, z), jnp.where(lo, z, pw)], axis=0)

    for key in pairs:
        e = pk[key]
        e["tinv"] = eye - e["pw"]
        pw = e["pw"].astype(BF16)
        e["pw"] = f32dot(pw, bd_of(pw))
    for level in range(1, 6):
        for key in pairs:
            e = pk[key]
            pw = e["pw"].astype(BF16)
            if level < 5:
                r = f32dot(jnp.concatenate([pw, e["tinv"].astype(BF16)], axis=0), bd_of(pw))
                e["pw"] = r[:CHUNK]
                e["tinv"] = e["tinv"] + r[CHUNK:]
            else:
                e["tinv"] = e["tinv"] + f32dot(e["tinv"].astype(BF16), bd_of(pw))

    for key in pairs:
        e = pk[key]
        halves = lambda t: (t[:, 0:1], t[:, LANES - 1:LANES])
        b, g, dk = halves(e["beta"]), halves(e["egc"]), halves(e["dk"])
        rhs = [jnp.concatenate([e["v"][hh] * b[hh], e["k"][hh] * (b[hh] * g[hh])], axis=1).astype(BF16)
               for hh in range(2)]
        uw = f32dot(e["tinv"].astype(BF16), _block_diag(rhs[0], rhs[1]))
        e["u"] = (uw[:, 0:hw], uw[:, 2 * hw:3 * hw])
        e["w"] = (uw[:, hw:2 * hw], uw[:, 3 * hw:4 * hw])
        e["qg"] = (e["q"][0] * g[0], e["q"][1] * g[1])
        e["kg"] = (e["k"][0] * dk[0], e["k"][1] * dk[1])
        e["sdec"] = (jnp.exp(e["g_last"][:, 0:1]), jnp.exp(e["g_last"][:, LANES - 1:LANES]))

    for j in range(cb):
        step = [(0, j, p) for p in range(npair)] + [(1, cb - 1 - j, p) for p in range(npair)]
        s_old, wsqs, v_new = {}, {}, {}
        for d, c, p in step:
            e = pk[d, c, p]
            for hh in range(2):
                s_old[d, p, hh] = s_ref[d, 2 * p + hh]
                wsqs[d, p, hh] = _bdot(jnp.concatenate([e["w"][hh], e["qg"][hh]], axis=0), s_old[d, p, hh])
        for d, c, p in step:
            for hh in range(2):
                v_new[d, p, hh] = pk[d, c, p]["u"][hh] - wsqs[d, p, hh][:CHUNK]
        for d, c, p in step:
            e = pk[d, c, p]
            bdv = _block_diag(v_new[d, p, 0].astype(BF16), v_new[d, p, 1].astype(BF16))
            o_pair = f32dot(e["qk"].astype(BF16), bdv)
            o_pair = o_pair + jnp.concatenate([wsqs[d, p, 0][CHUNK:], wsqs[d, p, 1][CHUNK:]], axis=1)
            refs[d][5][0, c * CHUNK:(c + 1) * CHUNK, 2 * p * hw:(2 * p + 2) * hw] = o_pair
            for hh in range(2):
                s_ref[d, 2 * p + hh] = s_old[d, p, hh] * e["sdec"][hh] + _bdot_tn(e["kg"][hh], v_new[d, p, hh])


def _gdn_scan(q, k, v, gate, gate_t, a_log, dt_bias, batch):
    n = q.shape[0]
    seq = n // batch
    tb = GDN_BLOCK_CHUNKS * CHUNK
    nblk = seq // tb
    q3, k3, v3 = (t.reshape(batch, seq, GDN_WIDTH) for t in (q, k, v))
    g3 = gate.reshape(batch, seq, LANES)
    gt4 = jnp.transpose(gate_t.reshape(16, batch, seq // CHUNK, CHUNK), (1, 2, 0, 3))
    arow = jnp.zeros((1, LANES), F32).at[0, 8:16].set(a_log.reshape(-1))
    drow = jnp.zeros((1, LANES), F32).at[0, 8:16].set(dt_bias.reshape(-1))
    acol = arow[0, :16].reshape(16, 1)
    dcol = drow[0, :16].reshape(16, 1)
    fwd = lambda w: pl.BlockSpec((1, tb, w), lambda b, i: (b, i, 0))
    bwd = lambda w: pl.BlockSpec((1, tb, w), lambda b, i: (b, nblk - 1 - i, 0))
    gtf = pl.BlockSpec((1, GDN_BLOCK_CHUNKS, 16, CHUNK), lambda b, i: (b, i, 0, 0))
    gtb = pl.BlockSpec((1, GDN_BLOCK_CHUNKS, 16, CHUNK), lambda b, i: (b, nblk - 1 - i, 0, 0))
    const = lambda s: pl.BlockSpec(s, lambda b, i: (0, 0))
    o_f, o_b = pl.pallas_call(
        _gdn_scan_kernel, grid=(batch, nblk),
        in_specs=[fwd(GDN_WIDTH), fwd(GDN_WIDTH), fwd(GDN_WIDTH), fwd(LANES), gtf,
                  bwd(GDN_WIDTH), bwd(GDN_WIDTH), bwd(GDN_WIDTH), bwd(LANES), gtb,
                  const((1, LANES)), const((1, LANES)), const((16, 1)), const((16, 1))],
        out_specs=[fwd(GDN_WIDTH), bwd(GDN_WIDTH)],
        out_shape=[jax.ShapeDtypeStruct((batch, seq, GDN_WIDTH), F32)] * 2,
        scratch_shapes=[pltpu.VMEM((2, GDN_HEADS, GDN_HEAD_DIM, GDN_HEAD_DIM), F32)],
        compiler_params=_cparams(("parallel", "arbitrary")),
        name="gdn_scan")(q3, k3, v3, g3, gt4, q3, k3, v3, g3, gt4, arow, drow, acol, dcol)
    return o_f.reshape(n, GDN_WIDTH), o_b.reshape(n, GDN_WIDTH)


def _outproj_kernel(x_ref, na_ref, of_ref, ob_ref, z_ref, gn_ref, w_ref, ln_ref, wr_ref, x1_out, h_out, aff_out):
    o = of_ref[...] + ob_ref[...]
    z = z_ref[...]
    gn = gn_ref[...]
    acc = jnp.dot(na_ref[...], w_ref[0:NA_WIDTH, :], preferred_element_type=F32)
    for hd in range(GDN_HEADS):
        sl = slice(hd * GDN_HEAD_DIM, (hd + 1) * GDN_HEAD_DIM)
        oh = o[:, sl]
        oh = oh * lax.rsqrt(jnp.mean(oh * oh, axis=-1, keepdims=True) + EPS) * gn
        oh = oh * _silu(z[:, sl])
        acc = acc + jnp.dot(oh.astype(BF16), w_ref[NA_WIDTH + hd * GDN_HEAD_DIM:NA_WIDTH + (hd + 1) * GDN_HEAD_DIM, :],
                            preferred_element_type=F32)
    x1 = x_ref[...] + acc
    x1_out[...] = x1
    h = _rms(x1, ln_ref[...])
    h_out[...] = h.astype(BF16)
    h1, h2, h3 = _split3(h)
    w1, w2, w3 = _split3(wr_ref[...])
    nt = lambda a, b: lax.dot_general(a, b, (((1,), (1,)), ((), ())), preferred_element_type=F32)
    logits = (nt(w1, h1) + (nt(w1, h2) + nt(w2, h1)) + (nt(w1, h3) + nt(w2, h2) + nt(w3, h1)))
    m = jnp.max(logits, axis=0, keepdims=True)
    e = jnp.exp(logits - m)
    aff = e / jnp.sum(e, axis=0, keepdims=True)
    for t in range(ROW_TILE // MOE_TOKEN_TILE):
        aff_out[t] = aff[:, t * MOE_TOKEN_TILE:(t + 1) * MOE_TOKEN_TILE]


def _outproj(x, na, o_f, o_b, z, gdn_norm, w_out, ln2, w_router_t):
    n = x.shape[0]
    tm = ROW_TILE
    sub = tm // MOE_TOKEN_TILE
    row = lambda w: pl.BlockSpec((tm, w), lambda i: (i, 0))
    const = lambda s: pl.BlockSpec(s, lambda i: (0, 0))
    return pl.pallas_call(
        _outproj_kernel, grid=(n // tm,),
        in_specs=[row(D_MODEL), row(NA_WIDTH), row(GDN_WIDTH), row(GDN_WIDTH), row(GDN_WIDTH),
                  const((1, GDN_HEAD_DIM)), const((D_MODEL, D_MODEL)), const((1, D_MODEL)), const((N_EXPERTS, D_MODEL))],
        out_specs=[row(D_MODEL), row(D_MODEL),
                   pl.BlockSpec((sub, N_EXPERTS, MOE_TOKEN_TILE), lambda i: (i, 0, 0))],
        out_shape=[jax.ShapeDtypeStruct((n, D_MODEL), F32), jax.ShapeDtypeStruct((n, D_MODEL), BF16),
                   jax.ShapeDtypeStruct((n // MOE_TOKEN_TILE, N_EXPERTS, MOE_TOKEN_TILE), F32)],
        compiler_params=_cparams(("parallel",)),
        name="outproj_router")(x, na, o_f, o_b, z, gdn_norm, w_out, ln2, w_router_t)


def _select_kernel(cap, aff_ref, lr_out, gate_out, tbl_out):
    aff = aff_ref[...]
    nt = aff.shape[0]
    count = lambda m: jnp.sum(jnp.sum(jnp.where(m, 1.0, 0.0), axis=0), axis=-1, keepdims=True)
    capf = jnp.float32(cap)
    as_f32 = lambda b: pltpu.bitcast(b, F32)

    def bit_step(i, thr):
        cand = jnp.bitwise_or(thr, jnp.left_shift(jnp.int32(1), 30 - i))
        return jnp.where(count(aff >= as_f32(cand)[None]) >= capf, cand, thr)

    thr = lax.fori_loop(0, 31, bit_step, jnp.zeros((N_EXPERTS, 1), jnp.int32))
    lo, hi = as_f32(thr), as_f32(thr + 1)

    def mid_step(_, lh):
        lo, hi = lh
        mid = 0.5 * (lo + hi)
        up = count(aff >= mid[None]) >= capf
        return jnp.where(up, mid, lo), jnp.where(up, hi, mid)

    lo, hi = lax.fori_loop(0, 32, mid_step, (lo, hi))
    gt = aff >= hi[None]
    eq = (aff >= lo[None]) & jnp.logical_not(gt)
    need = capf - count(gt)
    tok = (lax.broadcasted_iota(jnp.int32, aff.shape, 0) * MOE_TOKEN_TILE
           + lax.broadcasted_iota(jnp.int32, aff.shape, 2))

    def tie_step(i, bound):
        cand = bound + jnp.left_shift(jnp.int32(1), 15 - i)
        return jnp.where(count(eq & (tok < cand[None])) <= need, cand, bound)

    bound = lax.fori_loop(0, 16, tie_step, jnp.zeros((N_EXPERTS, 1), jnp.int32))
    mask = gt | (eq & (tok < bound[None]))
    maskf = jnp.where(mask, 1.0, 0.0)
    t_r = lax.broadcasted_iota(jnp.int32, (MOE_TOKEN_TILE, MOE_TOKEN_TILE), 0)
    t_c = lax.broadcasted_iota(jnp.int32, (MOE_TOKEN_TILE, MOE_TOKEN_TILE), 1)
    before = jnp.where(t_r < t_c, 1.0, 0.0).astype(BF16)
    rank = jnp.dot(maskf.astype(BF16).reshape(nt * N_EXPERTS, MOE_TOKEN_TILE), before,
                   preferred_element_type=F32).reshape(aff.shape)
    lr_out[...] = jnp.where(mask, rank + 1.0, 0.0)
    gate_out[...] = jnp.where(mask, aff, 0.0)
    cnt = jnp.sum(maskf, axis=-1, keepdims=True).astype(jnp.int32)
    padded = jnp.left_shift(jnp.right_shift(cnt + (MOE_SLOT_ALIGN - 1), MOE_SLOT_SHIFT), MOE_SLOT_SHIFT)
    run = padded
    s = 1
    while s < nt:
        run = run + jnp.concatenate([jnp.zeros((s,) + run.shape[1:], jnp.int32), run[:nt - s]], axis=0)
        s *= 2
    base = run - padded
    lane = lax.broadcasted_iota(jnp.int32, tbl_out.shape, 2)
    tbl_out[...] = jnp.where(lane == 0, base, jnp.where(lane == 1, cnt, 0))


def _select(aff3, cap):
    nt = aff3.shape[0]
    full = lambda w: pl.BlockSpec((nt, N_EXPERTS, w), lambda: (0, 0, 0))
    return pl.pallas_call(
        functools.partial(_select_kernel, cap), in_specs=[full(MOE_TOKEN_TILE)],
        out_specs=[full(MOE_TOKEN_TILE), full(MOE_TOKEN_TILE), full(LANES)],
        out_shape=[jax.ShapeDtypeStruct(aff3.shape, F32), jax.ShapeDtypeStruct(aff3.shape, F32),
                   jax.ShapeDtypeStruct((nt, N_EXPERTS, LANES), jnp.int32)],
        compiler_params=pltpu.CompilerParams(vmem_limit_bytes=VMEM_LIMIT),
        name="moe_select")(aff3)


def _slot_one_hot(lr, rnd, val):
    p1 = (lax.broadcasted_iota(jnp.int32, (MOE_SLOTS, MOE_TOKEN_TILE), 0) + 1 + rnd * MOE_SLOTS).astype(F32)
    rows = []
    for e in range(N_EXPERTS):
        v = 1.0 if val is None else val[e:e + 1, :]
        rows.append(jnp.where(lr[e:e + 1, :] == p1, v, 0.0))
    return jnp.concatenate(rows, axis=0)


def _dispatch_kernel(base_ref, cnt_ref, h_ref, lr_ref, xe_hbm_in, xe_hbm, stage, sems):
    del xe_hbm_in
    j = pl.program_id(0)
    lr = lr_ref[0]
    h = h_ref[...]
    rounds = (cnt_ref[j, 0] + MOE_SLOTS - 1) // MOE_SLOTS
    for e in range(1, N_EXPERTS):
        rounds = jnp.maximum(rounds, (cnt_ref[j, e] + MOE_SLOTS - 1) // MOE_SLOTS)

    def copy(e, rnd):
        dst = pl.multiple_of(base_ref[j, e] + rnd * MOE_SLOTS, MOE_SLOT_ALIGN)
        return pltpu.make_async_copy(stage.at[e], xe_hbm.at[e, pl.ds(dst, MOE_SLOTS)], sems.at[e])

    def one_round(rnd):
        oh = _slot_one_hot(lr, rnd, None).astype(BF16)
        rows = jnp.dot(oh, h, preferred_element_type=F32).astype(BF16)
        stage[...] = rows.reshape(N_EXPERTS, MOE_SLOTS, D_MODEL)
        for e in range(N_EXPERTS):
            @pl.when(cnt_ref[j, e] > rnd * MOE_SLOTS)
            def _():
                copy(e, rnd).start()
        for e in range(N_EXPERTS):
            @pl.when(cnt_ref[j, e] > rnd * MOE_SLOTS)
            def _():
                copy(e, rnd).wait()

    one_round(0)

    @pl.when(rounds > 1)
    def _():
        lax.fori_loop(1, rounds, lambda r, c: (one_round(r), c)[1], 0)


def _dispatch(base, cnt, h_bf, lr, rows_alloc):
    n = h_bf.shape[0]
    nt = n // MOE_TOKEN_TILE
    xe0 = jnp.zeros((N_EXPERTS, rows_alloc, D_MODEL), BF16)
    return pl.pallas_call(
        _dispatch_kernel,
        grid_spec=pltpu.PrefetchScalarGridSpec(
            num_scalar_prefetch=2, grid=(nt,),
            in_specs=[pl.BlockSpec((MOE_TOKEN_TILE, D_MODEL), lambda j, b, c: (j, 0)),
                      pl.BlockSpec((1, N_EXPERTS, MOE_TOKEN_TILE), lambda j, b, c: (j, 0, 0)),
                      pl.BlockSpec(memory_space=pl.ANY)],
            out_specs=pl.BlockSpec(memory_space=pl.ANY),
            scratch_shapes=[pltpu.VMEM((N_EXPERTS, MOE_SLOTS, D_MODEL), BF16),
                            pltpu.SemaphoreType.DMA((N_EXPERTS,))]),
        out_shape=jax.ShapeDtypeStruct(xe0.shape, BF16),
        input_output_aliases={4: 0},
        compiler_params=_cparams(("arbitrary",)),
        name="moe_dispatch")(base, cnt, h_bf, lr, xe0)


def _ffn_tiles(rows_ref, e):
    return (rows_ref[e] + MOE_SLOTS + FFN_ROW_TILE - 1) // FFN_ROW_TILE


def _ffn_kernel(rows_ref, x_ref, wg_ref, wu_ref, wd_ref, y_out):
    used = pl.program_id(1) < _ffn_tiles(rows_ref, pl.program_id(0))

    @pl.when(used)
    def _():
        x = x_ref[0]
        a = jnp.dot(x, wg_ref[0], preferred_element_type=F32)
        b = jnp.dot(x, wu_ref[0], preferred_element_type=F32)
        hid = (_silu(a) * b).astype(BF16)
        y_out[0] = jnp.dot(hid, wd_ref[0], preferred_element_type=F32).astype(y_out.dtype)

    @pl.when(jnp.logical_not(used))
    def _():
        y_out[...] = jnp.zeros_like(y_out)


def _expert_ffn(rows, xe, w_gate, w_up, w_down):
    e, r, _ = xe.shape
    tm = FFN_ROW_TILE
    tile = lambda ei, j, rows_ref: (ei, jnp.minimum(j, _ffn_tiles(rows_ref, ei) - 1), 0)
    wspec = pl.BlockSpec((1, D_MODEL, EXPERT_FF), lambda ei, j, rows_ref: (ei, 0, 0))
    return pl.pallas_call(
        _ffn_kernel,
        grid_spec=pltpu.PrefetchScalarGridSpec(
            num_scalar_prefetch=1, grid=(e, r // tm),
            in_specs=[pl.BlockSpec((1, tm, D_MODEL), tile), wspec, wspec,
                      pl.BlockSpec((1, EXPERT_FF, D_MODEL), lambda ei, j, rows_ref: (ei, 0, 0))],
            out_specs=pl.BlockSpec((1, tm, D_MODEL), lambda ei, j, rows_ref: (ei, j, 0))),
        out_shape=jax.ShapeDtypeStruct((e, r, D_MODEL), BF16),
        compiler_params=_cparams(("parallel", "arbitrary")),
        name="expert_ffn")(rows, xe, w_gate, w_up, w_down)


def _combine_kernel(final, base_ref, cnt_ref, x_ref, lr_ref, gate_ref, ye_hbm, *rest):
    if final:
        ln_ref, o_ref, slab, sems = rest
    else:
        o_ref, slab, sems = rest
    j = pl.program_id(0)
    lr = lr_ref[0]
    gate = gate_ref[0]
    g_hi = gate.astype(BF16).astype(F32)
    g_lo = gate - g_hi
    rounds = (cnt_ref[j, 0] + MOE_SLOTS - 1) // MOE_SLOTS
    for e in range(1, N_EXPERTS):
        rounds = jnp.maximum(rounds, (cnt_ref[j, e] + MOE_SLOTS - 1) // MOE_SLOTS)

    def copy(e, rnd):
        src = pl.multiple_of(base_ref[j, e] + rnd * MOE_SLOTS, MOE_SLOT_ALIGN)
        return pltpu.make_async_copy(ye_hbm.at[e, pl.ds(src, MOE_SLOTS)], slab.at[e], sems.at[e])

    def one_round(rnd):
        for e in range(N_EXPERTS):
            @pl.when(cnt_ref[j, e] > rnd * MOE_SLOTS)
            def _():
                copy(e, rnd).start()
        w_hi = _slot_one_hot(lr, rnd, g_hi).astype(BF16)
        w_lo = _slot_one_hot(lr, rnd, g_lo).astype(BF16)
        for e in range(N_EXPERTS):
            @pl.when(cnt_ref[j, e] > rnd * MOE_SLOTS)
            def _():
                copy(e, rnd).wait()

            @pl.when(cnt_ref[j, e] <= rnd * MOE_SLOTS)
            def _():
                slab[e] = jnp.zeros((MOE_SLOTS, D_MODEL), BF16)
        ye = slab[...].reshape(N_EXPERTS * MOE_SLOTS, D_MODEL)
        tn = lambda a, b: lax.dot_general(a, b, (((0,), (0,)), ((), ())), preferred_element_type=F32)
        return tn(w_hi, ye) + tn(w_lo, ye)

    y = one_round(0)
    y = lax.cond(rounds > 1, lambda: lax.fori_loop(1, rounds, lambda r, acc: acc + one_round(r), y), lambda: y)
    x2 = x_ref[...] + y
    o_ref[...] = _rms(x2, ln_ref[...]) if final else x2


def _combine(base, cnt, x1, lr, gate3, ye, ln_f):
    n = x1.shape[0]
    nt = n // MOE_TOKEN_TILE
    final = ln_f is not None
    tok = pl.BlockSpec((MOE_TOKEN_TILE, D_MODEL), lambda j, b, c: (j, 0))
    sel = pl.BlockSpec((1, N_EXPERTS, MOE_TOKEN_TILE), lambda j, b, c: (j, 0, 0))
    in_specs = [tok, sel, sel, pl.BlockSpec(memory_space=pl.ANY)]
    args = [base, cnt, x1, lr, gate3, ye]
    if final:
        in_specs.append(pl.BlockSpec((1, D_MODEL), lambda j, b, c: (0, 0)))
        args.append(ln_f)
    return pl.pallas_call(
        functools.partial(_combine_kernel, final),
        grid_spec=pltpu.PrefetchScalarGridSpec(
            num_scalar_prefetch=2, grid=(nt,), in_specs=in_specs, out_specs=tok,
            scratch_shapes=[pltpu.VMEM((N_EXPERTS, MOE_SLOTS, D_MODEL), BF16),
                            pltpu.SemaphoreType.DMA((N_EXPERTS,))]),
        out_shape=jax.ShapeDtypeStruct((n, D_MODEL), F32),
        compiler_params=_cparams(("arbitrary",)),
        name="moe_combine_final" if final else "moe_combine")(*args)


def _moe(x1, h_bf, aff3, wg, wu, wd, ln_f):
    n = h_bf.shape[0]
    nt = n // MOE_TOKEN_TILE
    cap = EC_CAPACITY * n // N_EXPERTS
    lr, gate3, tbl = _select(aff3, cap)
    base, cnt = tbl[:, :, 0], tbl[:, :, 1]
    last_pad = ((cnt[-1] + MOE_SLOT_ALIGN - 1) // MOE_SLOT_ALIGN) * MOE_SLOT_ALIGN
    rows = base[-1] + last_pad
    worst = cap + nt * (MOE_SLOT_ALIGN - 1) + MOE_SLOTS
    rows_alloc = -(-worst // FFN_ROW_TILE) * FFN_ROW_TILE
    xe = _dispatch(base, cnt, h_bf, lr, rows_alloc)
    ye = _expert_ffn(rows, xe, wg, wu, wd)
    return _combine(base, cnt, x1, lr, gate3, ye, ln_f)


def _prep_layer(l, ln1, w_in, conv_w, a_log, dt_bias, gdn_norm, rpb, w_out, ln2, w_router, w_gate, w_up, w_down):
    w_pad = jnp.pad(w_in[l], ((0, 0), (0, IN_WIDTH_PAD - IN_WIDTH))).astype(BF16)
    return dict(
        ln1=ln1[l].reshape(1, D_MODEL), w_pad=w_pad,
        wgt=jnp.transpose(w_in[l][:, GATE_COL0:IN_WIDTH]).astype(BF16),
        conv_w=jnp.pad(conv_w[l], ((0, 8 - CONV_K), (0, 0))),
        a_log=a_log[l], dt_bias=dt_bias[l], gdn_norm=gdn_norm[l].reshape(1, GDN_HEAD_DIM),
        bias_tab=_na_bias_table(rpb[l]), w_out=w_out[l].astype(BF16), ln2=ln2[l].reshape(1, D_MODEL),
        w_router_t=jnp.transpose(w_router[l]),
        wg=w_gate[l].astype(BF16), wu=w_up[l].astype(BF16), wd=w_down[l].astype(BF16))


def _trunk(x3, layers, ln_f):
    batch, seq, _ = x3.shape
    x = x3.reshape(batch * seq, D_MODEL)
    for li, p in enumerate(layers):
        q, k, v, c, z, gate, gate_t = _inproj(x, p["ln1"], p["w_pad"], p["wgt"])
        na = _neighbourhood_attention(q, k, v, p["bias_tab"], batch)
        gq, gk, gv = _gdn_prep(c, p["conv_w"], batch)
        o_f, o_b = _gdn_scan(gq, gk, gv, gate, gate_t, p["a_log"], p["dt_bias"], batch)
        x1, h_bf, aff3 = _outproj(x, na, o_f, o_b, z, p["gdn_norm"], p["w_out"], p["ln2"], p["w_router_t"])
        last = li == len(layers) - 1
        x = _moe(x1, h_bf, aff3, p["wg"], p["wu"], p["wd"], ln_f.reshape(1, D_MODEL) if last else None)
    return x.reshape(batch, seq, D_MODEL)


def kernel(x_prompt, x_sample, ln1, w_in, conv_w, a_log, dt_bias, gdn_norm, rpb, w_out, ln2, w_router, w_gate, w_up, w_down, ln_f):
    layers = [_prep_layer(l, ln1, w_in, conv_w, a_log, dt_bias, gdn_norm, rpb, w_out, ln2, w_router,
                          w_gate, w_up, w_down) for l in range(ln1.shape[0])]
    return (_trunk(x_prompt, layers, ln_f), _trunk(x_sample, layers, ln_f))
```

```python
import functools

import jax
import jax.numpy as jnp
from jax import lax
from jax.experimental import pallas as pl
from jax.experimental.pallas import tpu as pltpu

F32 = jnp.float32
BF16 = jnp.bfloat16

D_MODEL = 1024
GRID_W = 64
NA_HEADS = 8
NA_HEAD_DIM = 64
NA_WIDTH = 512
NA_ROWS = 8
NA_COLS = 16
GDN_HEADS = 4
GDN_HEAD_DIM = 128
GDN_WIDTH = 512
CONV_K = 5
CHUNK = 64
IN_WIDTH = 3600
N_EXPERTS = 16
EXPERT_FF = 1024
EC_CAPACITY = 2
EPS = 1e-6

LANES = 128
IN_WIDTH_PAD = 3712
GATE_COL0 = 3584
NEG_BIG = -1e30
VMEM_LIMIT = 56 * 1024 * 1024

ROW_TILE = 512
NA_ROW_BLOCK = 8
GDN_BLOCK_CHUNKS = 4
FFN_ROW_TILE = 512
MOE_TOKEN_TILE = 256
MOE_SLOTS = 64
MOE_SLOT_SHIFT = 4
MOE_SLOT_ALIGN = 1 << MOE_SLOT_SHIFT


def _cparams(sem):
    return pltpu.CompilerParams(dimension_semantics=sem, vmem_limit_bytes=VMEM_LIMIT)


def _bdot(a, b):
    return jnp.dot(a.astype(BF16), b.astype(BF16), preferred_element_type=F32)


def _bdot_nt(a, b):
    return lax.dot_general(a.astype(BF16), b.astype(BF16), (((1,), (1,)), ((), ())),
                           preferred_element_type=F32)


def _bdot_tn(a, b):
    return lax.dot_general(a.astype(BF16), b.astype(BF16), (((0,), (0,)), ((), ())),
                           preferred_element_type=F32)


def _split3(x):
    x1 = x.astype(BF16)
    r1 = x - x1.astype(F32)
    x2 = r1.astype(BF16)
    r2 = r1 - x2.astype(F32)
    return x1, x2, r2.astype(BF16)


def _silu(x):
    return x * (1.0 / (1.0 + jnp.exp(-x)))


def _sigmoid(x):
    return 1.0 / (1.0 + jnp.exp(-x))


def _softplus(x):
    return jnp.maximum(x, 0.0) + jnp.log(1.0 + jnp.exp(-jnp.abs(x)))


def _rms(x, g):
    ms = jnp.mean(x * x, axis=-1, keepdims=True)
    return (x * lax.rsqrt(ms + EPS)) * g


def _block_diag(a, b):
    za, zb = jnp.zeros_like(a), jnp.zeros_like(b)
    return jnp.concatenate([jnp.concatenate([a, zb], axis=1), jnp.concatenate([za, b], axis=1)], axis=0)


def _inproj_kernel(x_ref, ln_ref, w_ref, wgt_ref, q_out, k_out, v_out, c_out, z_out, g_out, gt_out):
    h = _rms(x_ref[...], ln_ref[...]).astype(BF16)
    q_out[...] = jnp.dot(h, w_ref[:, 0:512], preferred_element_type=F32).astype(BF16)
    k_out[...] = jnp.dot(h, w_ref[:, 512:1024], preferred_element_type=F32).astype(BF16)
    v_out[...] = jnp.dot(h, w_ref[:, 1024:1536], preferred_element_type=F32).astype(BF16)
    c_out[...] = jnp.dot(h, w_ref[:, 1536:3072], preferred_element_type=F32)
    z_out[...] = jnp.dot(h, w_ref[:, 3072:3584], preferred_element_type=F32)
    g_out[...] = jnp.dot(h, w_ref[:, GATE_COL0:IN_WIDTH_PAD], preferred_element_type=F32)
    gt_out[...] = lax.dot_general(wgt_ref[...], h, (((1,), (1,)), ((), ())), preferred_element_type=F32)


def _inproj(x, ln, w_pad, wgt):
    n = x.shape[0]
    tm = ROW_TILE
    row = lambda w: pl.BlockSpec((tm, w), lambda i: (i, 0))
    const = lambda s: pl.BlockSpec(s, lambda i: (0, 0))
    out_shape = [
        jax.ShapeDtypeStruct((n, NA_WIDTH), BF16), jax.ShapeDtypeStruct((n, NA_WIDTH), BF16),
        jax.ShapeDtypeStruct((n, NA_WIDTH), BF16), jax.ShapeDtypeStruct((n, 3 * GDN_WIDTH), F32),
        jax.ShapeDtypeStruct((n, GDN_WIDTH), F32), jax.ShapeDtypeStruct((n, LANES), F32),
        jax.ShapeDtypeStruct((16, n), F32)]
    out_specs = [row(NA_WIDTH), row(NA_WIDTH), row(NA_WIDTH), row(3 * GDN_WIDTH), row(GDN_WIDTH), row(LANES),
                 pl.BlockSpec((16, tm), lambda i: (0, i))]
    return pl.pallas_call(
        _inproj_kernel, grid=(n // tm,),
        in_specs=[row(D_MODEL), const((1, D_MODEL)), const((D_MODEL, IN_WIDTH_PAD)), const((16, D_MODEL))],
        out_specs=out_specs, out_shape=out_shape, compiler_params=_cparams(("parallel",)),
        name="inproj")(x, ln, w_pad, wgt)


def _na_kernel(rows, q_ref, k_ref, v_ref, bias_ref, o_ref):
    i = pl.program_id(2)
    lane = lax.broadcasted_iota(jnp.int32, (GRID_W, LANES), 1)
    low = lane < NA_HEAD_DIM
    scale = NA_HEAD_DIM ** -0.5

    def scores(rr):
        r = i * NA_ROW_BLOCK + rr
        rs = jnp.clip(r - NA_ROWS // 2, 0, rows - NA_ROWS)
        d0 = rs - r + NA_ROWS - 1
        k0 = pl.multiple_of(rs * GRID_W, GRID_W)
        kw = k_ref[0, pl.ds(k0, NA_ROWS * GRID_W), :]
        q = q_ref[0, rr * GRID_W:(rr + 1) * GRID_W, :]
        zq = jnp.zeros_like(q)
        q2 = jnp.concatenate([jnp.where(low, q, zq), jnp.where(low, zq, q)], axis=0)
        s = _bdot_nt(q2, kw) * scale
        bias = jnp.concatenate([bias_ref[0, d0], bias_ref[1, d0]], axis=0)
        return s + bias, k0

    def finish(rr, s, k0):
        vw = v_ref[0, pl.ds(k0, NA_ROWS * GRID_W), :]
        m = jnp.max(s, axis=-1, keepdims=True)
        p = jnp.exp(s - m)
        inv = 1.0 / jnp.sum(p, axis=-1, keepdims=True)
        o2 = _bdot(p, vw) * inv
        o_ref[0, rr * GRID_W:(rr + 1) * GRID_W, :] = jnp.where(low, o2[:GRID_W], o2[GRID_W:]).astype(o_ref.dtype)

    pending = scores(0)
    for rr in range(NA_ROW_BLOCK):
        nxt = scores(rr + 1) if rr + 1 < NA_ROW_BLOCK else None
        finish(rr, *pending)
        pending = nxt


def _na_bias_table(rpb):
    col = jnp.arange(GRID_W)
    cs = jnp.clip(col - NA_COLS // 2, 0, GRID_W - NA_COLS)
    kcol = jnp.arange(GRID_W)
    valid = (kcol[None, :] >= cs[:, None]) & (kcol[None, :] < cs[:, None] + NA_COLS)
    cidx = jnp.clip(kcol[None, :] - col[:, None] + NA_COLS - 1, 0, 2 * NA_COLS - 2)
    band = jnp.where(valid[None, None], rpb[:, :, cidx], NEG_BIG)
    ridx = jnp.arange(NA_ROWS)[:, None] + jnp.arange(NA_ROWS)[None, :]
    tab = band[:, ridx]
    tab = jnp.transpose(tab, (0, 1, 3, 2, 4))
    return tab.reshape(NA_HEADS, NA_ROWS, GRID_W, NA_ROWS * GRID_W).astype(F32)


def _neighbourhood_attention(q, k, v, bias_tab, batch):
    n = q.shape[0]
    seq = n // batch
    rows = seq // GRID_W
    q3, k3, v3 = (t.reshape(batch, seq, NA_WIDTH) for t in (q, k, v))
    tq = NA_ROW_BLOCK * GRID_W
    out = pl.pallas_call(
        functools.partial(_na_kernel, rows),
        grid=(batch, NA_HEADS // 2, rows // NA_ROW_BLOCK),
        in_specs=[pl.BlockSpec((1, tq, LANES), lambda b, hp, i: (b, i, hp)),
                  pl.BlockSpec((1, seq, LANES), lambda b, hp, i: (b, 0, hp)),
                  pl.BlockSpec((1, seq, LANES), lambda b, hp, i: (b, 0, hp)),
                  pl.BlockSpec((2, NA_ROWS, GRID_W, NA_ROWS * GRID_W), lambda b, hp, i: (hp, 0, 0, 0))],
        out_specs=pl.BlockSpec((1, tq, LANES), lambda b, hp, i: (b, i, hp)),
        out_shape=jax.ShapeDtypeStruct((batch, seq, NA_WIDTH), BF16),
        compiler_params=_cparams(("parallel", "parallel", "arbitrary")),
        name="na_attention")(q3, k3, v3, bias_tab)
    return out.reshape(n, NA_WIDTH)


def _gdn_prep_kernel(nblk, c_ref, prev_ref, next_ref, w_ref, q_out, k_out, v_out):
    i = pl.program_id(1)
    tm = c_ref.shape[1]
    pad = CONV_K // 2
    prev = jnp.where(i > 0, prev_ref[0], 0.0)
    nxt = jnp.where(i < nblk - 1, next_ref[0], 0.0)
    xc = jnp.concatenate([prev, c_ref[0], nxt], axis=0)
    acc = xc[8 - pad:8 - pad + tm] * w_ref[0:1, :]
    for j in range(1, CONV_K):
        acc = acc + xc[8 - pad + j:8 - pad + j + tm] * w_ref[j:j + 1, :]
    c = _silu(acc)
    for hd in range(GDN_HEADS):
        sl = slice(hd * GDN_HEAD_DIM, (hd + 1) * GDN_HEAD_DIM)
        qh = c[:, sl]
        kh = c[:, GDN_WIDTH + hd * GDN_HEAD_DIM:GDN_WIDTH + (hd + 1) * GDN_HEAD_DIM]
        q_out[0, :, sl] = qh * lax.rsqrt(jnp.sum(qh * qh, axis=-1, keepdims=True) + EPS)
        k_out[0, :, sl] = kh * lax.rsqrt(jnp.sum(kh * kh, axis=-1, keepdims=True) + EPS)
    v_out[0] = c[:, 2 * GDN_WIDTH:]


def _gdn_prep(c, conv_w, batch):
    n = c.shape[0]
    seq = n // batch
    tm = ROW_TILE
    nblk = seq // tm
    c3 = c.reshape(batch, seq, 3 * GDN_WIDTH)
    hb = tm // 8
    spec = pl.BlockSpec((1, tm, GDN_WIDTH), lambda b, i: (b, i, 0))
    outs = pl.pallas_call(
        functools.partial(_gdn_prep_kernel, nblk),
        grid=(batch, nblk),
        in_specs=[pl.BlockSpec((1, tm, 3 * GDN_WIDTH), lambda b, i: (b, i, 0)),
                  pl.BlockSpec((1, 8, 3 * GDN_WIDTH), lambda b, i: (b, jnp.maximum(i * hb - 1, 0), 0)),
                  pl.BlockSpec((1, 8, 3 * GDN_WIDTH), lambda b, i: (b, jnp.minimum((i + 1) * hb, seq // 8 - 1), 0)),
                  pl.BlockSpec((8, 3 * GDN_WIDTH), lambda b, i: (0, 0))],
        out_specs=[spec, spec, spec],
        out_shape=[jax.ShapeDtypeStruct((batch, seq, GDN_WIDTH), F32)] * 3,
        compiler_params=_cparams(("parallel", "parallel")),
        name="gdn_prep")(c3, c3, c3, conv_w)
    return tuple(t.reshape(n, GDN_WIDTH) for t in outs)


def _gdn_scan_kernel(qf, kf, vf, gf, gtf, qb, kb, vb, gb, gtb, arow, drow, acol, dcol, of_ref, ob_ref, s_ref):
    @pl.when(pl.program_id(1) == 0)
    def _():
        s_ref[...] = jnp.zeros_like(s_ref)

    cb = GDN_BLOCK_CHUNKS
    hw = GDN_HEAD_DIM
    npair = GDN_HEADS // 2
    ri = lax.broadcasted_iota(jnp.int32, (CHUNK, LANES), 0)
    lane = lax.broadcasted_iota(jnp.int32, (CHUNK, LANES), 1)
    lo = lane < CHUNK
    cm = jnp.bitwise_and(lane, CHUNK - 1)
    eye = jnp.where(ri == cm, 1.0, 0.0)
    rr = lax.broadcasted_iota(jnp.int32, (CHUNK, CHUNK), 0)
    cc = lax.broadcasted_iota(jnp.int32, (CHUNK, CHUNK), 1)
    neg_a_row, neg_a_col = -jnp.exp(arow[...]), -jnp.exp(acol[...])
    d_row, d_col = drow[...], dcol[...]
    refs = ((qf, kf, vf, gf, gtf, of_ref), (qb, kb, vb, gb, gtb, ob_ref))
    incl = (ri >= cm, ri <= cm)
    strict = (ri > cm, ri < cm)
    tri = (jnp.where(rr >= cc, 1.0, 0.0).astype(BF16), jnp.where(rr <= cc, 1.0, 0.0).astype(BF16))
    tri_t2 = (jnp.where(cm >= ri, 1.0, 0.0).astype(BF16), jnp.where(cm <= ri, 1.0, 0.0).astype(BF16))
    f32dot = lambda a, b: jnp.dot(a, b, preferred_element_type=F32)

    inst = [(d, c) for d in (0, 1) for c in range(cb)]
    pairs = [(d, c, p) for d, c in inst for p in range(npair)]

    st = {}
    for d, c in inst:
        q_ref, k_ref, v_ref, g_ref, gt_ref, _ = refs[d]
        sl = slice(c * CHUNK, (c + 1) * CHUNK)
        graw = g_ref[0, sl, :]
        g1, g2, g3 = _split3(neg_a_row * _softplus(graw + d_row))
        t1, t2, t3 = _split3(neg_a_col * _softplus(gt_ref[0, c] + d_col))
        st[d, c] = dict(
            gc_full=f32dot(tri[d], g1) + f32dot(tri[d], g2) + f32dot(tri[d], g3),
            gct2=f32dot(t1, tri_t2[d]) + f32dot(t2, tri_t2[d]) + f32dot(t3, tri_t2[d]),
            beta_full=_sigmoid(graw),
            q=q_ref[0, sl, :], k=k_ref[0, sl, :], v=v_ref[0, sl, :])

    pk = {}
    for d, c, p in pairs:
        s = st[d, c]
        h0, h1 = 2 * p, 2 * p + 1
        col = lambda full, base: jnp.where(lo, full[:, base + h0:base + h0 + 1], full[:, base + h1:base + h1 + 1])
        gc = col(s["gc_full"], 8 + 4 * d)
        beta = col(s["beta_full"], 4 * d)
        g_row0 = 8 + 4 * d + h0
        gr = jnp.where(lo[0:1], s["gct2"][g_row0:g_row0 + 1, :], s["gct2"][g_row0 + 1:g_row0 + 2, :])
        hs = lambda t, h: t[:, h * hw:(h + 1) * hw]
        q0, q1 = hs(s["q"], h0) * (hw ** -0.5), hs(s["q"], h1) * (hw ** -0.5)
        k0, k1 = hs(s["k"], h0), hs(s["k"], h1)
        lhs = jnp.concatenate([jnp.concatenate([q0, k0], axis=0), jnp.concatenate([q1, k1], axis=0)], axis=1)
        qkk = _bdot_nt(lhs, _block_diag(k0.astype(BF16), k1.astype(BF16)))
        decay = jnp.where(incl[d], jnp.exp(jnp.where(incl[d], gc - gr, 0.0)), 0.0)
        last = CHUNK - 1 if d == 0 else 0
        g_last = gc[last:last + 1, :]
        pk[d, c, p] = dict(qk=qkk[:CHUNK] * decay, pw=jnp.where(strict[d], qkk[CHUNK:] * beta * decay, 0.0),
                           beta=beta, egc=jnp.exp(gc), g_last=g_last, dk=jnp.exp(g_last - gc),
                           q=(q0, q1), k=(k0, k1), v=(hs(s["v"], h0), hs(s["v"], h1)))

    def bd_of(pw):
        z = jnp.zeros_like(pw)
        return jnp.concatenate([jnp.where(lo, pw, z), jnp.where(lo, z, pw)], axis=0)

    for key in pairs:
        e = pk[key]
        e["tinv"] = eye - e["pw"]
        pw = e["pw"].astype(BF16)
        e["pw"] = f32dot(pw, bd_of(pw))
    for level in range(1, 6):
        for key in pairs:
            e = pk[key]
            pw = e["pw"].astype(BF16)
            if level < 5:
                r = f32dot(jnp.concatenate([pw, e["tinv"].astype(BF16)], axis=0), bd_of(pw))
                e["pw"] = r[:CHUNK]
                e["tinv"] = e["tinv"] + r[CHUNK:]
            else:
                e["tinv"] = e["tinv"] + f32dot(e["tinv"].astype(BF16), bd_of(pw))

    for key in pairs:
        e = pk[key]
        halves = lambda t: (t[:, 0:1], t[:, LANES - 1:LANES])
        b, g, dk = halves(e["beta"]), halves(e["egc"]), halves(e["dk"])
        rhs = [jnp.concatenate([e["v"][hh] * b[hh], e["k"][hh] * (b[hh] * g[hh])], axis=1).astype(BF16)
               for hh in range(2)]
        uw = f32dot(e["tinv"].astype(BF16), _block_diag(rhs[0], rhs[1]))
        e["u"] = (uw[:, 0:hw], uw[:, 2 * hw:3 * hw])
        e["w"] = (uw[:, hw:2 * hw], uw[:, 3 * hw:4 * hw])
        e["qg"] = (e["q"][0] * g[0], e["q"][1] * g[1])
        e["kg"] = (e["k"][0] * dk[0], e["k"][1] * dk[1])
        e["sdec"] = (jnp.exp(e["g_last"][:, 0:1]), jnp.exp(e["g_last"][:, LANES - 1:LANES]))

    for j in range(cb):
        step = [(0, j, p) for p in range(npair)] + [(1, cb - 1 - j, p) for p in range(npair)]
        s_old, wsqs, v_new = {}, {}, {}
        for d, c, p in step:
            e = pk[d, c, p]
            for hh in range(2):
                s_old[d, p, hh] = s_ref[d, 2 * p + hh]
                wsqs[d, p, hh] = _bdot(jnp.concatenate([e["w"][hh], e["qg"][hh]], axis=0), s_old[d, p, hh])
        for d, c, p in step:
            for hh in range(2):
                v_new[d, p, hh] = pk[d, c, p]["u"][hh] - wsqs[d, p, hh][:CHUNK]
        for d, c, p in step:
            e = pk[d, c, p]
            bdv = _block_diag(v_new[d, p, 0].astype(BF16), v_new[d, p, 1].astype(BF16))
            o_pair = f32dot(e["qk"].astype(BF16), bdv)
            o_pair = o_pair + jnp.concatenate([wsqs[d, p, 0][CHUNK:], wsqs[d, p, 1][CHUNK:]], axis=1)
            refs[d][5][0, c * CHUNK:(c + 1) * CHUNK, 2 * p * hw:(2 * p + 2) * hw] = o_pair
            for hh in range(2):
                s_ref[d, 2 * p + hh] = s_old[d, p, hh] * e["sdec"][hh] + _bdot_tn(e["kg"][hh], v_new[d, p, hh])


def _gdn_scan(q, k, v, gate, gate_t, a_log, dt_bias, batch):
    n = q.shape[0]
    seq = n // batch
    tb = GDN_BLOCK_CHUNKS * CHUNK
    nblk = seq // tb
    q3, k3, v3 = (t.reshape(batch, seq, GDN_WIDTH) for t in (q, k, v))
    g3 = gate.reshape(batch, seq, LANES)
    gt4 = jnp.transpose(gate_t.reshape(16, batch, seq // CHUNK, CHUNK), (1, 2, 0, 3))
    arow = jnp.zeros((1, LANES), F32).at[0, 8:16].set(a_log.reshape(-1))
    drow = jnp.zeros((1, LANES), F32).at[0, 8:16].set(dt_bias.reshape(-1))
    acol = arow[0, :16].reshape(16, 1)
    dcol = drow[0, :16].reshape(16, 1)
    fwd = lambda w: pl.BlockSpec((1, tb, w), lambda b, i: (b, i, 0))
    bwd = lambda w: pl.BlockSpec((1, tb, w), lambda b, i: (b, nblk - 1 - i, 0))
    gtf = pl.BlockSpec((1, GDN_BLOCK_CHUNKS, 16, CHUNK), lambda b, i: (b, i, 0, 0))
    gtb = pl.BlockSpec((1, GDN_BLOCK_CHUNKS, 16, CHUNK), lambda b, i: (b, nblk - 1 - i, 0, 0))
    const = lambda s: pl.BlockSpec(s, lambda b, i: (0, 0))
    o_f, o_b = pl.pallas_call(
        _gdn_scan_kernel, grid=(batch, nblk),
        in_specs=[fwd(GDN_WIDTH), fwd(GDN_WIDTH), fwd(GDN_WIDTH), fwd(LANES), gtf,
                  bwd(GDN_WIDTH), bwd(GDN_WIDTH), bwd(GDN_WIDTH), bwd(LANES), gtb,
                  const((1, LANES)), const((1, LANES)), const((16, 1)), const((16, 1))],
        out_specs=[fwd(GDN_WIDTH), bwd(GDN_WIDTH)],
        out_shape=[jax.ShapeDtypeStruct((batch, seq, GDN_WIDTH), F32)] * 2,
        scratch_shapes=[pltpu.VMEM((2, GDN_HEADS, GDN_HEAD_DIM, GDN_HEAD_DIM), F32)],
        compiler_params=_cparams(("parallel", "arbitrary")),
        name="gdn_scan")(q3, k3, v3, g3, gt4, q3, k3, v3, g3, gt4, arow, drow, acol, dcol)
    return o_f.reshape(n, GDN_WIDTH), o_b.reshape(n, GDN_WIDTH)


def _outproj_kernel(x_ref, na_ref, of_ref, ob_ref, z_ref, gn_ref, w_ref, ln_ref, wr_ref, x1_out, h_out, aff_out):
    o = of_ref[...] + ob_ref[...]
    z = z_ref[...]
    gn = gn_ref[...]
    acc = jnp.dot(na_ref[...], w_ref[0:NA_WIDTH, :], preferred_element_type=F32)
    for hd in range(GDN_HEADS):
        sl = slice(hd * GDN_HEAD_DIM, (hd + 1) * GDN_HEAD_DIM)
        oh = o[:, sl]
        oh = oh * lax.rsqrt(jnp.mean(oh * oh, axis=-1, keepdims=True) + EPS) * gn
        oh = oh * _silu(z[:, sl])
        acc = acc + jnp.dot(oh.astype(BF16), w_ref[NA_WIDTH + hd * GDN_HEAD_DIM:NA_WIDTH + (hd + 1) * GDN_HEAD_DIM, :],
                            preferred_element_type=F32)
    x1 = x_ref[...] + acc
    x1_out[...] = x1
    h = _rms(x1, ln_ref[...])
    h_out[...] = h.astype(BF16)
    h1, h2, h3 = _split3(h)
    w1, w2, w3 = _split3(wr_ref[...])
    nt = lambda a, b: lax.dot_general(a, b, (((1,), (1,)), ((), ())), preferred_element_type=F32)
    logits = (nt(w1, h1) + (nt(w1, h2) + nt(w2, h1)) + (nt(w1, h3) + nt(w2, h2) + nt(w3, h1)))
    m = jnp.max(logits, axis=0, keepdims=True)
    e = jnp.exp(logits - m)
    aff = e / jnp.sum(e, axis=0, keepdims=True)
    for t in range(ROW_TILE // MOE_TOKEN_TILE):
        aff_out[t] = aff[:, t * MOE_TOKEN_TILE:(t + 1) * MOE_TOKEN_TILE]


def _outproj(x, na, o_f, o_b, z, gdn_norm, w_out, ln2, w_router_t):
    n = x.shape[0]
    tm = ROW_TILE
    sub = tm // MOE_TOKEN_TILE
    row = lambda w: pl.BlockSpec((tm, w), lambda i: (i, 0))
    const = lambda s: pl.BlockSpec(s, lambda i: (0, 0))
    return pl.pallas_call(
        _outproj_kernel, grid=(n // tm,),
        in_specs=[row(D_MODEL), row(NA_WIDTH), row(GDN_WIDTH), row(GDN_WIDTH), row(GDN_WIDTH),
                  const((1, GDN_HEAD_DIM)), const((D_MODEL, D_MODEL)), const((1, D_MODEL)), const((N_EXPERTS, D_MODEL))],
        out_specs=[row(D_MODEL), row(D_MODEL),
                   pl.BlockSpec((sub, N_EXPERTS, MOE_TOKEN_TILE), lambda i: (i, 0, 0))],
        out_shape=[jax.ShapeDtypeStruct((n, D_MODEL), F32), jax.ShapeDtypeStruct((n, D_MODEL), BF16),
                   jax.ShapeDtypeStruct((n // MOE_TOKEN_TILE, N_EXPERTS, MOE_TOKEN_TILE), F32)],
        compiler_params=_cparams(("parallel",)),
        name="outproj_router")(x, na, o_f, o_b, z, gdn_norm, w_out, ln2, w_router_t)


def _select_kernel(cap, aff_ref, lr_out, gate_out, tbl_out):
    aff = aff_ref[...]
    nt = aff.shape[0]
    count = lambda m: jnp.sum(jnp.sum(jnp.where(m, 1.0, 0.0), axis=0), axis=-1, keepdims=True)
    capf = jnp.float32(cap)
    as_f32 = lambda b: pltpu.bitcast(b, F32)

    def bit_step(i, thr):
        cand = jnp.bitwise_or(thr, jnp.left_shift(jnp.int32(1), 30 - i))
        return jnp.where(count(aff >= as_f32(cand)[None]) >= capf, cand, thr)

    thr = lax.fori_loop(0, 31, bit_step, jnp.zeros((N_EXPERTS, 1), jnp.int32))
    lo, hi = as_f32(thr), as_f32(thr + 1)

    def mid_step(_, lh):
        lo, hi = lh
        mid = 0.5 * (lo + hi)
        up = count(aff >= mid[None]) >= capf
        return jnp.where(up, mid, lo), jnp.where(up, hi, mid)

    lo, hi = lax.fori_loop(0, 32, mid_step, (lo, hi))
    gt = aff >= hi[None]
    eq = (aff >= lo[None]) & jnp.logical_not(gt)
    need = capf - count(gt)
    tok = (lax.broadcasted_iota(jnp.int32, aff.shape, 0) * MOE_TOKEN_TILE
           + lax.broadcasted_iota(jnp.int32, aff.shape, 2))

    def tie_step(i, bound):
        cand = bound + jnp.left_shift(jnp.int32(1), 15 - i)
        return jnp.where(count(eq & (tok < cand[None])) <= need, cand, bound)

    bound = lax.fori_loop(0, 16, tie_step, jnp.zeros((N_EXPERTS, 1), jnp.int32))
    mask = gt | (eq & (tok < bound[None]))
    maskf = jnp.where(mask, 1.0, 0.0)
    t_r = lax.broadcasted_iota(jnp.int32, (MOE_TOKEN_TILE, MOE_TOKEN_TILE), 0)
    t_c = lax.broadcasted_iota(jnp.int32, (MOE_TOKEN_TILE, MOE_TOKEN_TILE), 1)
    before = jnp.where(t_r < t_c, 1.0, 0.0).astype(BF16)
    rank = jnp.dot(maskf.astype(BF16).reshape(nt * N_EXPERTS, MOE_TOKEN_TILE), before,
                   preferred_element_type=F32).reshape(aff.shape)
    lr_out[...] = jnp.where(mask, rank + 1.0, 0.0)
    gate_out[...] = jnp.where(mask, aff, 0.0)
    cnt = jnp.sum(maskf, axis=-1, keepdims=True).astype(jnp.int32)
    run = cnt
    s = 1
    while s < nt:
        run = run + jnp.concatenate([jnp.zeros((s,) + run.shape[1:], jnp.int32), run[:nt - s]], axis=0)
        s *= 2
    lane = lax.broadcasted_iota(jnp.int32, tbl_out.shape, 2)
    tbl_out[...] = jnp.where(lane == 0, run - cnt, jnp.where(lane == 1, cnt, 0))


def _select(aff3, cap):
    nt = aff3.shape[0]
    full = lambda w: pl.BlockSpec((nt, N_EXPERTS, w), lambda: (0, 0, 0))
    return pl.pallas_call(
        functools.partial(_select_kernel, cap), in_specs=[full(MOE_TOKEN_TILE)],
        out_specs=[full(MOE_TOKEN_TILE), full(MOE_TOKEN_TILE), full(LANES)],
        out_shape=[jax.ShapeDtypeStruct(aff3.shape, F32), jax.ShapeDtypeStruct(aff3.shape, F32),
                   jax.ShapeDtypeStruct((nt, N_EXPERTS, LANES), jnp.int32)],
        compiler_params=pltpu.CompilerParams(vmem_limit_bytes=VMEM_LIMIT),
        name="moe_select")(aff3)


def _slot_geometry(start_ref, cnt_ref, j, e):
    start, cnt = start_ref[j, e], cnt_ref[j, e]
    shift = jnp.bitwise_and(start, MOE_SLOT_ALIGN - 1)
    return start - shift, shift, cnt, shift + cnt


def _slot_rounds(start_ref, cnt_ref, j):
    rounds = jnp.int32(0)
    for e in range(N_EXPERTS):
        _, _, cnt, total = _slot_geometry(start_ref, cnt_ref, j, e)
        rounds = jnp.maximum(rounds, jnp.where(cnt > 0, (total + MOE_SLOTS - 1) // MOE_SLOTS, 0))
    return rounds


def _slot_one_hot(lr, shifts, rnd, val):
    p1 = (lax.broadcasted_iota(jnp.int32, (MOE_SLOTS, MOE_TOKEN_TILE), 0) + 1 + rnd * MOE_SLOTS).astype(F32)
    rows = []
    for e in range(N_EXPERTS):
        lre = lr[e:e + 1, :]
        pos = jnp.where(lre > 0.0, lre + shifts[e].astype(F32), 0.0)
        v = 1.0 if val is None else val[e:e + 1, :]
        rows.append(jnp.where(pos == p1, v, 0.0))
    return jnp.concatenate(rows, axis=0)


def _dispatch_kernel(cap, start_ref, cnt_ref, h_ref, lr_ref, xe_hbm, stage, tail, sems):
    j = pl.program_id(0)
    nt = pl.num_programs(0)
    buf = lax.rem(j, 2)
    lr = lr_ref[0]
    h = h_ref[...]
    geo = [_slot_geometry(start_ref, cnt_ref, j, e) for e in range(N_EXPERTS)]
    rounds = _slot_rounds(start_ref, cnt_ref, j)

    def copy(jj, e, rnd, b):
        abase = _slot_geometry(start_ref, cnt_ref, jj, e)[0]
        dst = pl.multiple_of(abase + rnd * MOE_SLOTS, MOE_SLOT_ALIGN)
        return pltpu.make_async_copy(stage.at[b, e, pl.ds(0, MOE_SLOTS)], xe_hbm.at[e, pl.ds(dst, MOE_SLOTS)],
                                     sems.at[b, e])

    @pl.when(j == 0)
    def _():
        tail[...] = jnp.zeros_like(tail)
        stage[...] = jnp.zeros_like(stage)
        for e in range(N_EXPERTS):
            pltpu.make_async_copy(stage.at[1, e, pl.ds(0, MOE_SLOTS)], xe_hbm.at[e, pl.ds(cap, MOE_SLOTS)],
                                  sems.at[1, e]).start()
        for e in range(N_EXPERTS):
            pltpu.make_async_copy(stage.at[1, e, pl.ds(0, MOE_SLOTS)], xe_hbm.at[e, pl.ds(cap, MOE_SLOTS)],
                                  sems.at[1, e]).wait()

    def fill(rnd, first):
        oh = _slot_one_hot(lr, [g[1] for g in geo], rnd, None).astype(BF16)
        rows = jnp.dot(oh, h, preferred_element_type=F32)
        for e in range(N_EXPERTS):
            r = rows[e * MOE_SLOTS:(e + 1) * MOE_SLOTS]
            if first:
                r = jnp.concatenate([r[:MOE_SLOT_ALIGN] + tail[e].astype(F32), r[MOE_SLOT_ALIGN:]], axis=0)
            stage[buf, e, 0:MOE_SLOTS, :] = r.astype(BF16)
        for e in range(N_EXPERTS):
            _, _, cnt, total = geo[e]

            @pl.when((cnt > 0) & ((total - 1) // MOE_SLOTS == rnd))
            def _():
                grp = jnp.right_shift(total, MOE_SLOT_SHIFT) - rnd * (MOE_SLOTS // MOE_SLOT_ALIGN)
                tail[e] = stage[buf, e, pl.ds(pl.multiple_of(grp * MOE_SLOT_ALIGN, MOE_SLOT_ALIGN), MOE_SLOT_ALIGN), :]

    def each_active(jj, rnd, fn):
        for e in range(N_EXPERTS):
            _, _, cnt, total = _slot_geometry(start_ref, cnt_ref, jj, e)

            @pl.when((cnt > 0) & (total > rnd * MOE_SLOTS))
            def _():
                fn(e)

    fill(0, True)
    jp = jnp.maximum(j - 1, 0)

    @pl.when((j > 0) & (_slot_rounds(start_ref, cnt_ref, jp) <= 1))
    def _():
        each_active(jp, 0, lambda e: copy(jp, e, 0, 1 - buf).wait())

    each_active(j, 0, lambda e: copy(j, e, 0, buf).start())

    @pl.when(rounds > 1)
    def _():
        each_active(j, 0, lambda e: copy(j, e, 0, buf).wait())

        def more(r, c):
            fill(r, False)
            each_active(j, r, lambda e: copy(j, e, r, buf).start())
            each_active(j, r, lambda e: copy(j, e, r, buf).wait())
            return c

        lax.fori_loop(1, rounds, more, 0)

    @pl.when((j == nt - 1) & (rounds <= 1))
    def _():
        each_active(j, 0, lambda e: copy(j, e, 0, buf).wait())


def _dispatch(start, cnt, h_bf, lr, cap):
    n = h_bf.shape[0]
    nt = n // MOE_TOKEN_TILE
    return pl.pallas_call(
        functools.partial(_dispatch_kernel, cap),
        grid_spec=pltpu.PrefetchScalarGridSpec(
            num_scalar_prefetch=2, grid=(nt,),
            in_specs=[pl.BlockSpec((MOE_TOKEN_TILE, D_MODEL), lambda j, b, c: (j, 0)),
                      pl.BlockSpec((1, N_EXPERTS, MOE_TOKEN_TILE), lambda j, b, c: (j, 0, 0))],
            out_specs=pl.BlockSpec(memory_space=pl.ANY),
            scratch_shapes=[pltpu.VMEM((2, N_EXPERTS, MOE_SLOTS + MOE_SLOT_ALIGN, D_MODEL), BF16),
                            pltpu.VMEM((N_EXPERTS, MOE_SLOT_ALIGN, D_MODEL), BF16),
                            pltpu.SemaphoreType.DMA((2, N_EXPERTS))]),
        out_shape=jax.ShapeDtypeStruct((N_EXPERTS, cap + MOE_SLOTS, D_MODEL), BF16),
        compiler_params=_cparams(("arbitrary",)),
        name="moe_dispatch")(start, cnt, h_bf, lr)


def _ffn_kernel(ntiles, x_ref, wg_ref, wu_ref, wd_ref, y_out):
    @pl.when(pl.program_id(1) < ntiles)
    def _():
        x = x_ref[0]
        a = jnp.dot(x, wg_ref[0], preferred_element_type=F32)
        b = jnp.dot(x, wu_ref[0], preferred_element_type=F32)
        hid = (_silu(a) * b).astype(BF16)
        y_out[0] = jnp.dot(hid, wd_ref[0], preferred_element_type=F32).astype(y_out.dtype)

    @pl.when(pl.program_id(1) == ntiles)
    def _():
        y_out[...] = jnp.zeros_like(y_out)


def _expert_ffn(xe, cap, w_gate, w_up, w_down):
    tm = FFN_ROW_TILE
    ntiles = cap // tm
    wspec = pl.BlockSpec((1, D_MODEL, EXPERT_FF), lambda ei, j: (ei, 0, 0))
    return pl.pallas_call(
        functools.partial(_ffn_kernel, ntiles), grid=(N_EXPERTS, ntiles + 1),
        in_specs=[pl.BlockSpec((1, tm, D_MODEL), lambda ei, j: (ei, jnp.minimum(j, ntiles - 1), 0)), wspec, wspec,
                  pl.BlockSpec((1, EXPERT_FF, D_MODEL), lambda ei, j: (ei, 0, 0))],
        out_specs=pl.BlockSpec((1, tm, D_MODEL), lambda ei, j: (ei, j, 0)),
        out_shape=jax.ShapeDtypeStruct((N_EXPERTS, cap + tm, D_MODEL), BF16),
        compiler_params=_cparams(("parallel", "arbitrary")),
        name="expert_ffn")(xe, w_gate, w_up, w_down)


def _combine_kernel(final, start_ref, cnt_ref, x_ref, lr_ref, gate_ref, ye_hbm, *rest):
    if final:
        ln_ref, o_ref, slab, sems = rest
    else:
        o_ref, slab, sems = rest
    j = pl.program_id(0)
    nt = pl.num_programs(0)
    buf = lax.rem(j, 2)
    extra = 2
    lr = lr_ref[0]
    gate = gate_ref[0]
    g_hi = gate.astype(BF16).astype(F32)
    g_lo = gate - g_hi
    shifts = [_slot_geometry(start_ref, cnt_ref, j, e)[1] for e in range(N_EXPERTS)]
    rounds = _slot_rounds(start_ref, cnt_ref, j)

    def copy(jj, e, rnd, b):
        abase = _slot_geometry(start_ref, cnt_ref, jj, e)[0]
        src = pl.multiple_of(abase + rnd * MOE_SLOTS, MOE_SLOT_ALIGN)
        return pltpu.make_async_copy(ye_hbm.at[e, pl.ds(src, MOE_SLOTS)], slab.at[b, e], sems.at[b, e])

    def each(jj, rnd, fn, fn_idle=None):
        for e in range(N_EXPERTS):
            _, _, cnt, total = _slot_geometry(start_ref, cnt_ref, jj, e)
            active = (cnt > 0) & (total > rnd * MOE_SLOTS)

            @pl.when(active)
            def _():
                fn(e)

            if fn_idle is not None:
                @pl.when(jnp.logical_not(active))
                def _():
                    fn_idle(e)

    def window(rnd, b):
        w_hi = _slot_one_hot(lr, shifts, rnd, g_hi).astype(BF16)
        w_lo = _slot_one_hot(lr, shifts, rnd, g_lo).astype(BF16)

        def idle(e):
            slab[b, e] = jnp.zeros((MOE_SLOTS, D_MODEL), BF16)

        each(j, rnd, lambda e: copy(j, e, rnd, b).wait(), idle)
        ye = slab[b].reshape(N_EXPERTS * MOE_SLOTS, D_MODEL)
        tn = lambda a, c: lax.dot_general(a, c, (((0,), (0,)), ((), ())), preferred_element_type=F32)
        return tn(w_hi, ye) + tn(w_lo, ye)

    @pl.when(j == 0)
    def _():
        each(j, 0, lambda e: copy(j, e, 0, buf).start())

    jn = jnp.minimum(j + 1, nt - 1)

    @pl.when(j + 1 < nt)
    def _():
        each(jn, 0, lambda e: copy(jn, e, 0, 1 - buf).start())

    y = window(0, buf)

    def more(r, acc):
        each(j, r, lambda e: copy(j, e, r, extra).start())
        return acc + window(r, extra)

    y = lax.cond(rounds > 1, lambda: lax.fori_loop(1, rounds, more, y), lambda: y)
    x2 = x_ref[...] + y
    o_ref[...] = _rms(x2, ln_ref[...]) if final else x2


def _combine(base, cnt, x1, lr, gate3, ye, ln_f):
    n = x1.shape[0]
    nt = n // MOE_TOKEN_TILE
    final = ln_f is not None
    tok = pl.BlockSpec((MOE_TOKEN_TILE, D_MODEL), lambda j, b, c: (j, 0))
    sel = pl.BlockSpec((1, N_EXPERTS, MOE_TOKEN_TILE), lambda j, b, c: (j, 0, 0))
    in_specs = [tok, sel, sel, pl.BlockSpec(memory_space=pl.ANY)]
    args = [base, cnt, x1, lr, gate3, ye]
    if final:
        in_specs.append(pl.BlockSpec((1, D_MODEL), lambda j, b, c: (0, 0)))
        args.append(ln_f)
    return pl.pallas_call(
        functools.partial(_combine_kernel, final),
        grid_spec=pltpu.PrefetchScalarGridSpec(
            num_scalar_prefetch=2, grid=(nt,), in_specs=in_specs, out_specs=tok,
            scratch_shapes=[pltpu.VMEM((3, N_EXPERTS, MOE_SLOTS, D_MODEL), BF16),
                            pltpu.SemaphoreType.DMA((3, N_EXPERTS))]),
        out_shape=jax.ShapeDtypeStruct((n, D_MODEL), F32),
        compiler_params=_cparams(("arbitrary",)),
        name="moe_combine_final" if final else "moe_combine")(*args)


def _moe(x1, h_bf, aff3, wg, wu, wd, ln_f):
    n = h_bf.shape[0]
    nt = n // MOE_TOKEN_TILE
    cap = EC_CAPACITY * n // N_EXPERTS
    lr, gate3, tbl = _select(aff3, cap)
    start, cnt = tbl[:, :, 0], tbl[:, :, 1]
    xe = _dispatch(start, cnt, h_bf, lr, cap)
    ye = _expert_ffn(xe, cap, wg, wu, wd)
    return _combine(start, cnt, x1, lr, gate3, ye, ln_f)


def _prep_layer(l, ln1, w_in, conv_w, a_log, dt_bias, gdn_norm, rpb, w_out, ln2, w_router, w_gate, w_up, w_down):
    w_pad = jnp.pad(w_in[l], ((0, 0), (0, IN_WIDTH_PAD - IN_WIDTH))).astype(BF16)
    return dict(
        ln1=ln1[l].reshape(1, D_MODEL), w_pad=w_pad,
        wgt=jnp.transpose(w_in[l][:, GATE_COL0:IN_WIDTH]).astype(BF16),
        conv_w=jnp.pad(conv_w[l], ((0, 8 - CONV_K), (0, 0))),
        a_log=a_log[l], dt_bias=dt_bias[l], gdn_norm=gdn_norm[l].reshape(1, GDN_HEAD_DIM),
        bias_tab=_na_bias_table(rpb[l]), w_out=w_out[l].astype(BF16), ln2=ln2[l].reshape(1, D_MODEL),
        w_router_t=jnp.transpose(w_router[l]),
        wg=w_gate[l].astype(BF16), wu=w_up[l].astype(BF16), wd=w_down[l].astype(BF16))


def _trunk(x3, layers, ln_f):
    batch, seq, _ = x3.shape
    x = x3.reshape(batch * seq, D_MODEL)
    for li, p in enumerate(layers):
        q, k, v, c, z, gate, gate_t = _inproj(x, p["ln1"], p["w_pad"], p["wgt"])
        na = _neighbourhood_attention(q, k, v, p["bias_tab"], batch)
        gq, gk, gv = _gdn_prep(c, p["conv_w"], batch)
        o_f, o_b = _gdn_scan(gq, gk, gv, gate, gate_t, p["a_log"], p["dt_bias"], batch)
        x1, h_bf, aff3 = _outproj(x, na, o_f, o_b, z, p["gdn_norm"], p["w_out"], p["ln2"], p["w_router_t"])
        last = li == len(layers) - 1
        x = _moe(x1, h_bf, aff3, p["wg"], p["wu"], p["wd"], ln_f.reshape(1, D_MODEL) if last else None)
    return x.reshape(batch, seq, D_MODEL)


def kernel(x_prompt, x_sample, ln1, w_in, conv_w, a_log, dt_bias, gdn_norm, rpb, w_out, ln2, w_router, w_gate, w_up, w_down, ln_f):
    layers = [_prep_layer(l, ln1, w_in, conv_w, a_log, dt_bias, gdn_norm, rpb, w_out, ln2, w_router,
                          w_gate, w_up, w_down) for l in range(ln1.shape[0])]
    return (_trunk(x_prompt, layers, ln_f), _trunk(x_sample, layers, ln_f))
```

```python
import functools

import jax
import jax.numpy as jnp
from jax import lax
from jax.experimental import pallas as pl
from jax.experimental.pallas import tpu as pltpu

F32 = jnp.float32
BF16 = jnp.bfloat16

D_MODEL = 1024
GRID_W = 64
NA_HEADS = 8
NA_HEAD_DIM = 64
NA_WIDTH = 512
NA_ROWS = 8
NA_COLS = 16
GDN_HEADS = 4
GDN_HEAD_DIM = 128
GDN_WIDTH = 512
CONV_K = 5
CHUNK = 64
IN_WIDTH = 3600
N_EXPERTS = 16
EXPERT_FF = 1024
EC_CAPACITY = 2
EPS = 1e-6

LANES = 128
IN_WIDTH_PAD = 3712
GATE_COL0 = 3584
NEG_BIG = -1e30
VMEM_LIMIT = 56 * 1024 * 1024

ROW_TILE = 512
NA_ROW_BLOCK = 16
GDN_BLOCK_CHUNKS = 4
FFN_ROW_TILE = 512
MOE_TOKEN_TILE = 256
MOE_SLOTS = 64
MOE_SLOT_SHIFT = 4
MOE_SLOT_ALIGN = 1 << MOE_SLOT_SHIFT


def _cparams(sem):
    return pltpu.CompilerParams(dimension_semantics=sem, vmem_limit_bytes=VMEM_LIMIT)


def _bdot(a, b):
    return jnp.dot(a.astype(BF16), b.astype(BF16), preferred_element_type=F32)


def _bdot_nt(a, b):
    return lax.dot_general(a.astype(BF16), b.astype(BF16), (((1,), (1,)), ((), ())),
                           preferred_element_type=F32)


def _bdot_tn(a, b):
    return lax.dot_general(a.astype(BF16), b.astype(BF16), (((0,), (0,)), ((), ())),
                           preferred_element_type=F32)


def _split3(x):
    x1 = x.astype(BF16)
    r1 = x - x1.astype(F32)
    x2 = r1.astype(BF16)
    r2 = r1 - x2.astype(F32)
    return x1, x2, r2.astype(BF16)


def _silu(x):
    return x * (1.0 / (1.0 + jnp.exp(-x)))


def _sigmoid(x):
    return 1.0 / (1.0 + jnp.exp(-x))


def _softplus(x):
    return jnp.maximum(x, 0.0) + jnp.log(1.0 + jnp.exp(-jnp.abs(x)))


def _rms(x, g):
    ms = jnp.mean(x * x, axis=-1, keepdims=True)
    return (x * lax.rsqrt(ms + EPS)) * g


def _block_diag(a, b):
    za, zb = jnp.zeros_like(a), jnp.zeros_like(b)
    return jnp.concatenate([jnp.concatenate([a, zb], axis=1), jnp.concatenate([za, b], axis=1)], axis=0)


def _inproj_kernel(x_ref, ln_ref, w_ref, wgt_ref, q_out, k_out, v_out, c_out, z_out, g_out, gt_out):
    h = _rms(x_ref[...], ln_ref[...]).astype(BF16)
    q_out[...] = jnp.dot(h, w_ref[:, 0:512], preferred_element_type=F32).astype(BF16)
    k_out[...] = jnp.dot(h, w_ref[:, 512:1024], preferred_element_type=F32).astype(BF16)
    v_out[...] = jnp.dot(h, w_ref[:, 1024:1536], preferred_element_type=F32).astype(BF16)
    c_out[...] = jnp.dot(h, w_ref[:, 1536:3072], preferred_element_type=F32)
    z_out[...] = jnp.dot(h, w_ref[:, 3072:3584], preferred_element_type=F32)
    g_out[...] = jnp.dot(h, w_ref[:, GATE_COL0:IN_WIDTH_PAD], preferred_element_type=F32)
    gt_out[...] = lax.dot_general(wgt_ref[...], h, (((1,), (1,)), ((), ())), preferred_element_type=F32)


def _inproj(x, ln, w_pad, wgt):
    n = x.shape[0]
    tm = ROW_TILE
    row = lambda w: pl.BlockSpec((tm, w), lambda i: (i, 0))
    const = lambda s: pl.BlockSpec(s, lambda i: (0, 0))
    out_shape = [
        jax.ShapeDtypeStruct((n, NA_WIDTH), BF16), jax.ShapeDtypeStruct((n, NA_WIDTH), BF16),
        jax.ShapeDtypeStruct((n, NA_WIDTH), BF16), jax.ShapeDtypeStruct((n, 3 * GDN_WIDTH), F32),
        jax.ShapeDtypeStruct((n, GDN_WIDTH), F32), jax.ShapeDtypeStruct((n, LANES), F32),
        jax.ShapeDtypeStruct((16, n), F32)]
    out_specs = [row(NA_WIDTH), row(NA_WIDTH), row(NA_WIDTH), row(3 * GDN_WIDTH), row(GDN_WIDTH), row(LANES),
                 pl.BlockSpec((16, tm), lambda i: (0, i))]
    return pl.pallas_call(
        _inproj_kernel, grid=(n // tm,),
        in_specs=[row(D_MODEL), const((1, D_MODEL)), const((D_MODEL, IN_WIDTH_PAD)), const((16, D_MODEL))],
        out_specs=out_specs, out_shape=out_shape, compiler_params=_cparams(("parallel",)),
        name="inproj")(x, ln, w_pad, wgt)


def _na_kernel(rows, q_ref, k_ref, v_ref, bias_ref, o_ref):
    i = pl.program_id(2)
    lane = lax.broadcasted_iota(jnp.int32, (GRID_W, LANES), 1)
    low = lane < NA_HEAD_DIM
    scale = NA_HEAD_DIM ** -0.5

    def scores(rr):
        r = i * NA_ROW_BLOCK + rr
        rs = jnp.clip(r - NA_ROWS // 2, 0, rows - NA_ROWS)
        d0 = rs - r + NA_ROWS - 1
        k0 = pl.multiple_of(rs * GRID_W, GRID_W)
        kw = k_ref[0, pl.ds(k0, NA_ROWS * GRID_W), :]
        q = q_ref[0, rr * GRID_W:(rr + 1) * GRID_W, :]
        zq = jnp.zeros_like(q)
        q2 = jnp.concatenate([jnp.where(low, q, zq), jnp.where(low, zq, q)], axis=0)
        s = _bdot_nt(q2, kw) * scale
        bias = jnp.concatenate([bias_ref[0, d0], bias_ref[1, d0]], axis=0)
        return s + bias, k0

    def finish(rr, s, k0):
        vw = v_ref[0, pl.ds(k0, NA_ROWS * GRID_W), :]
        m = jnp.max(s, axis=-1, keepdims=True)
        p = jnp.exp(s - m)
        inv = 1.0 / jnp.sum(p, axis=-1, keepdims=True)
        o2 = _bdot(p, vw) * inv
        o_ref[0, rr * GRID_W:(rr + 1) * GRID_W, :] = jnp.where(low, o2[:GRID_W], o2[GRID_W:]).astype(o_ref.dtype)

    pending = scores(0)
    for rr in range(NA_ROW_BLOCK):
        nxt = scores(rr + 1) if rr + 1 < NA_ROW_BLOCK else None
        finish(rr, *pending)
        pending = nxt


def _na_bias_table(rpb):
    col = jnp.arange(GRID_W)
    cs = jnp.clip(col - NA_COLS // 2, 0, GRID_W - NA_COLS)
    kcol = jnp.arange(GRID_W)
    valid = (kcol[None, :] >= cs[:, None]) & (kcol[None, :] < cs[:, None] + NA_COLS)
    cidx = jnp.clip(kcol[None, :] - col[:, None] + NA_COLS - 1, 0, 2 * NA_COLS - 2)
    band = jnp.where(valid[None, None], rpb[:, :, cidx], NEG_BIG)
    ridx = jnp.arange(NA_ROWS)[:, None] + jnp.arange(NA_ROWS)[None, :]
    tab = band[:, ridx]
    tab = jnp.transpose(tab, (0, 1, 3, 2, 4))
    return tab.reshape(NA_HEADS, NA_ROWS, GRID_W, NA_ROWS * GRID_W).astype(F32)


def _neighbourhood_attention(q, k, v, bias_tab, batch):
    n = q.shape[0]
    seq = n // batch
    rows = seq // GRID_W
    q3, k3, v3 = (t.reshape(batch, seq, NA_WIDTH) for t in (q, k, v))
    tq = NA_ROW_BLOCK * GRID_W
    out = pl.pallas_call(
        functools.partial(_na_kernel, rows),
        grid=(batch, NA_HEADS // 2, rows // NA_ROW_BLOCK),
        in_specs=[pl.BlockSpec((1, tq, LANES), lambda b, hp, i: (b, i, hp)),
                  pl.BlockSpec((1, seq, LANES), lambda b, hp, i: (b, 0, hp)),
                  pl.BlockSpec((1, seq, LANES), lambda b, hp, i: (b, 0, hp)),
                  pl.BlockSpec((2, NA_ROWS, GRID_W, NA_ROWS * GRID_W), lambda b, hp, i: (hp, 0, 0, 0))],
        out_specs=pl.BlockSpec((1, tq, LANES), lambda b, hp, i: (b, i, hp)),
        out_shape=jax.ShapeDtypeStruct((batch, seq, NA_WIDTH), BF16),
        compiler_params=_cparams(("parallel", "parallel", "arbitrary")),
        name="na_attention")(q3, k3, v3, bias_tab)
    return out.reshape(n, NA_WIDTH)


def _gdn_prep_kernel(nblk, c_ref, prev_ref, next_ref, w_ref, q_out, k_out, v_out, xs):
    i = pl.program_id(1)
    tm = c_ref.shape[1]
    pad = CONV_K // 2
    halo = 8
    xs[0:halo, :] = jnp.where(i > 0, prev_ref[0], 0.0)
    xs[halo:halo + tm, :] = c_ref[0]
    xs[halo + tm:, :] = jnp.where(i < nblk - 1, next_ref[0], 0.0)
    outs = (q_out, k_out, v_out)
    for blk in range(3 * GDN_HEADS):
        sl = slice(blk * GDN_HEAD_DIM, (blk + 1) * GDN_HEAD_DIM)
        acc = xs[halo - pad:halo - pad + tm, sl] * w_ref[0:1, sl]
        for j in range(1, CONV_K):
            acc = acc + xs[halo - pad + j:halo - pad + j + tm, sl] * w_ref[j:j + 1, sl]
        c = _silu(acc)
        which, hd = divmod(blk, GDN_HEADS)
        if which < 2:
            c = c * lax.rsqrt(jnp.sum(c * c, axis=-1, keepdims=True) + EPS)
        outs[which][0, :, hd * GDN_HEAD_DIM:(hd + 1) * GDN_HEAD_DIM] = c.astype(outs[which].dtype)


def _gdn_prep(c, conv_w, batch):
    n = c.shape[0]
    seq = n // batch
    tm = ROW_TILE
    nblk = seq // tm
    c3 = c.reshape(batch, seq, 3 * GDN_WIDTH)
    hb = tm // 8
    spec = pl.BlockSpec((1, tm, GDN_WIDTH), lambda b, i: (b, i, 0))
    outs = pl.pallas_call(
        functools.partial(_gdn_prep_kernel, nblk),
        grid=(batch, nblk),
        in_specs=[pl.BlockSpec((1, tm, 3 * GDN_WIDTH), lambda b, i: (b, i, 0)),
                  pl.BlockSpec((1, 8, 3 * GDN_WIDTH), lambda b, i: (b, jnp.maximum(i * hb - 1, 0), 0)),
                  pl.BlockSpec((1, 8, 3 * GDN_WIDTH), lambda b, i: (b, jnp.minimum((i + 1) * hb, seq // 8 - 1), 0)),
                  pl.BlockSpec((8, 3 * GDN_WIDTH), lambda b, i: (0, 0))],
        out_specs=[spec, spec, spec],
        out_shape=[jax.ShapeDtypeStruct((batch, seq, GDN_WIDTH), BF16)] * 3,
        scratch_shapes=[pltpu.VMEM((tm + 16, 3 * GDN_WIDTH), F32)],
        compiler_params=_cparams(("parallel", "parallel")),
        name="gdn_prep")(c3, c3, c3, conv_w)
    return tuple(t.reshape(n, GDN_WIDTH) for t in outs)


def _gdn_scan_kernel(qf, kf, vf, gf, gtf, qb, kb, vb, gb, gtb, arow, drow, acol, dcol, of_ref, ob_ref, s_ref):
    @pl.when(pl.program_id(1) == 0)
    def _():
        s_ref[...] = jnp.zeros_like(s_ref)

    cb = GDN_BLOCK_CHUNKS
    hw = GDN_HEAD_DIM
    npair = GDN_HEADS // 2
    ri = lax.broadcasted_iota(jnp.int32, (CHUNK, LANES), 0)
    lane = lax.broadcasted_iota(jnp.int32, (CHUNK, LANES), 1)
    lo = lane < CHUNK
    cm = jnp.bitwise_and(lane, CHUNK - 1)
    eye = jnp.where(ri == cm, 1.0, 0.0)
    rr = lax.broadcasted_iota(jnp.int32, (CHUNK, CHUNK), 0)
    cc = lax.broadcasted_iota(jnp.int32, (CHUNK, CHUNK), 1)
    neg_a_row, neg_a_col = -jnp.exp(arow[...]), -jnp.exp(acol[...])
    d_row, d_col = drow[...], dcol[...]
    refs = ((qf, kf, vf, gf, gtf, of_ref), (qb, kb, vb, gb, gtb, ob_ref))
    incl = (ri >= cm, ri <= cm)
    strict = (ri > cm, ri < cm)
    tri = (jnp.where(rr >= cc, 1.0, 0.0).astype(BF16), jnp.where(rr <= cc, 1.0, 0.0).astype(BF16))
    tri_t2 = (jnp.where(cm >= ri, 1.0, 0.0).astype(BF16), jnp.where(cm <= ri, 1.0, 0.0).astype(BF16))
    f32dot = lambda a, b: jnp.dot(a, b, preferred_element_type=F32)

    inst = [(d, c) for d in (0, 1) for c in range(cb)]
    pairs = [(d, c, p) for d, c in inst for p in range(npair)]

    st = {}
    for d, c in inst:
        q_ref, k_ref, v_ref, g_ref, gt_ref, _ = refs[d]
        sl = slice(c * CHUNK, (c + 1) * CHUNK)
        graw = g_ref[0, sl, :]
        g1, g2, g3 = _split3(neg_a_row * _softplus(graw + d_row))
        t1, t2, t3 = _split3(neg_a_col * _softplus(gt_ref[0, c] + d_col))
        st[d, c] = dict(
            gc_full=f32dot(tri[d], g1) + f32dot(tri[d], g2) + f32dot(tri[d], g3),
            gct2=f32dot(t1, tri_t2[d]) + f32dot(t2, tri_t2[d]) + f32dot(t3, tri_t2[d]),
            beta_full=_sigmoid(graw),
            q=q_ref[0, sl, :].astype(F32), k=k_ref[0, sl, :].astype(F32), v=v_ref[0, sl, :].astype(F32))

    pk = {}
    for d, c, p in pairs:
        s = st[d, c]
        h0, h1 = 2 * p, 2 * p + 1
        col = lambda full, base: jnp.where(lo, full[:, base + h0:base + h0 + 1], full[:, base + h1:base + h1 + 1])
        gc = col(s["gc_full"], 8 + 4 * d)
        beta = col(s["beta_full"], 4 * d)
        g_row0 = 8 + 4 * d + h0
        gr = jnp.where(lo[0:1], s["gct2"][g_row0:g_row0 + 1, :], s["gct2"][g_row0 + 1:g_row0 + 2, :])
        hs = lambda t, h: t[:, h * hw:(h + 1) * hw]
        q0, q1 = hs(s["q"], h0) * (hw ** -0.5), hs(s["q"], h1) * (hw ** -0.5)
        k0, k1 = hs(s["k"], h0), hs(s["k"], h1)
        lhs = jnp.concatenate([jnp.concatenate([q0, k0], axis=0), jnp.concatenate([q1, k1], axis=0)], axis=1)
        qkk = _bdot_nt(lhs, _block_diag(k0.astype(BF16), k1.astype(BF16)))
        decay = jnp.where(incl[d], jnp.exp(jnp.where(incl[d], gc - gr, 0.0)), 0.0)
        last = CHUNK - 1 if d == 0 else 0
        g_last = gc[last:last + 1, :]
        pk[d, c, p] = dict(qk=qkk[:CHUNK] * decay, pw=jnp.where(strict[d], qkk[CHUNK:] * beta * decay, 0.0),
                           beta=beta, egc=jnp.exp(gc), g_last=g_last, dk=jnp.exp(g_last - gc),
                           q=(q0, q1), k=(k0, k1), v=(hs(s["v"], h0), hs(s["v"], h1)))

    def bd_of(pw):
        z = jnp.zeros_like(pw)
        return jnp.concatenate([jnp.where(lo, pw, z), jnp.where(lo, z, pw)], axis=0)

    for key in pairs:
        e = pk[key]
        e["tinv"] = eye - e["pw"]
        pw = e["pw"].astype(BF16)
        e["pw"] = f32dot(pw, bd_of(pw))
    for level in range(1, 6):
        for key in pairs:
            e = pk[key]
            pw = e["pw"].astype(BF16)
            if level < 5:
                r = f32dot(jnp.concatenate([pw, e["tinv"].astype(BF16)], axis=0), bd_of(pw))
                e["pw"] = r[:CHUNK]
                e["tinv"] = e["tinv"] + r[CHUNK:]
            else:
                e["tinv"] = e["tinv"] + f32dot(e["tinv"].astype(BF16), bd_of(pw))

    for key in pairs:
        e = pk[key]
        halves = lambda t: (t[:, 0:1], t[:, LANES - 1:LANES])
        b, g, dk = halves(e["beta"]), halves(e["egc"]), halves(e["dk"])
        rhs = [jnp.concatenate([e["v"][hh] * b[hh], e["k"][hh] * (b[hh] * g[hh])], axis=1).astype(BF16)
               for hh in range(2)]
        uw = f32dot(e["tinv"].astype(BF16), _block_diag(rhs[0], rhs[1]))
        e["u"] = (uw[:, 0:hw], uw[:, 2 * hw:3 * hw])
        e["w"] = (uw[:, hw:2 * hw], uw[:, 3 * hw:4 * hw])
        e["qg"] = (e["q"][0] * g[0], e["q"][1] * g[1])
        e["kg"] = (e["k"][0] * dk[0], e["k"][1] * dk[1])
        e["sdec"] = (jnp.exp(e["g_last"][:, 0:1]), jnp.exp(e["g_last"][:, LANES - 1:LANES]))

    for j in range(cb):
        step = [(0, j, p) for p in range(npair)] + [(1, cb - 1 - j, p) for p in range(npair)]
        s_old, wsqs, v_new = {}, {}, {}
        for d, c, p in step:
            e = pk[d, c, p]
            for hh in range(2):
                s_old[d, p, hh] = s_ref[d, 2 * p + hh]
                wsqs[d, p, hh] = _bdot(jnp.concatenate([e["w"][hh], e["qg"][hh]], axis=0), s_old[d, p, hh])
        for d, c, p in step:
            for hh in range(2):
                v_new[d, p, hh] = pk[d, c, p]["u"][hh] - wsqs[d, p, hh][:CHUNK]
        for d, c, p in step:
            e = pk[d, c, p]
            bdv = _block_diag(v_new[d, p, 0].astype(BF16), v_new[d, p, 1].astype(BF16))
            o_pair = f32dot(e["qk"].astype(BF16), bdv)
            o_pair = o_pair + jnp.concatenate([wsqs[d, p, 0][CHUNK:], wsqs[d, p, 1][CHUNK:]], axis=1)
            refs[d][5][0, c * CHUNK:(c + 1) * CHUNK, 2 * p * hw:(2 * p + 2) * hw] = o_pair
            for hh in range(2):
                s_ref[d, 2 * p + hh] = s_old[d, p, hh] * e["sdec"][hh] + _bdot_tn(e["kg"][hh], v_new[d, p, hh])


def _gdn_scan(q, k, v, gate, gate_t, a_log, dt_bias, batch):
    n = q.shape[0]
    seq = n // batch
    tb = GDN_BLOCK_CHUNKS * CHUNK
    nblk = seq // tb
    q3, k3, v3 = (t.reshape(batch, seq, GDN_WIDTH) for t in (q, k, v))
    g3 = gate.reshape(batch, seq, LANES)
    gt4 = jnp.transpose(gate_t.reshape(16, batch, seq // CHUNK, CHUNK), (1, 2, 0, 3))
    arow = jnp.zeros((1, LANES), F32).at[0, 8:16].set(a_log.reshape(-1))
    drow = jnp.zeros((1, LANES), F32).at[0, 8:16].set(dt_bias.reshape(-1))
    acol = arow[0, :16].reshape(16, 1)
    dcol = drow[0, :16].reshape(16, 1)
    fwd = lambda w: pl.BlockSpec((1, tb, w), lambda b, i: (b, i, 0))
    bwd = lambda w: pl.BlockSpec((1, tb, w), lambda b, i: (b, nblk - 1 - i, 0))
    gtf = pl.BlockSpec((1, GDN_BLOCK_CHUNKS, 16, CHUNK), lambda b, i: (b, i, 0, 0))
    gtb = pl.BlockSpec((1, GDN_BLOCK_CHUNKS, 16, CHUNK), lambda b, i: (b, nblk - 1 - i, 0, 0))
    const = lambda s: pl.BlockSpec(s, lambda b, i: (0, 0))
    o_f, o_b = pl.pallas_call(
        _gdn_scan_kernel, grid=(batch, nblk),
        in_specs=[fwd(GDN_WIDTH), fwd(GDN_WIDTH), fwd(GDN_WIDTH), fwd(LANES), gtf,
                  bwd(GDN_WIDTH), bwd(GDN_WIDTH), bwd(GDN_WIDTH), bwd(LANES), gtb,
                  const((1, LANES)), const((1, LANES)), const((16, 1)), const((16, 1))],
        out_specs=[fwd(GDN_WIDTH), bwd(GDN_WIDTH)],
        out_shape=[jax.ShapeDtypeStruct((batch, seq, GDN_WIDTH), F32)] * 2,
        scratch_shapes=[pltpu.VMEM((2, GDN_HEADS, GDN_HEAD_DIM, GDN_HEAD_DIM), F32)],
        compiler_params=_cparams(("parallel", "arbitrary")),
        name="gdn_scan")(q3, k3, v3, g3, gt4, q3, k3, v3, g3, gt4, arow, drow, acol, dcol)
    return o_f.reshape(n, GDN_WIDTH), o_b.reshape(n, GDN_WIDTH)


def _outproj_kernel(x_ref, na_ref, of_ref, ob_ref, z_ref, gn_ref, w_ref, ln_ref, wr_ref, x1_out, h_out, aff_out):
    o = of_ref[...] + ob_ref[...]
    z = z_ref[...]
    gn = gn_ref[...]
    acc = jnp.dot(na_ref[...], w_ref[0:NA_WIDTH, :], preferred_element_type=F32)
    for hd in range(GDN_HEADS):
        sl = slice(hd * GDN_HEAD_DIM, (hd + 1) * GDN_HEAD_DIM)
        oh = o[:, sl]
        oh = oh * lax.rsqrt(jnp.mean(oh * oh, axis=-1, keepdims=True) + EPS) * gn
        oh = oh * _silu(z[:, sl])
        acc = acc + jnp.dot(oh.astype(BF16), w_ref[NA_WIDTH + hd * GDN_HEAD_DIM:NA_WIDTH + (hd + 1) * GDN_HEAD_DIM, :],
                            preferred_element_type=F32)
    x1 = x_ref[...] + acc
    x1_out[...] = x1
    h = _rms(x1, ln_ref[...])
    h_out[...] = h.astype(BF16)
    h1, h2, _ = _split3(h)
    w1, w2, _ = _split3(wr_ref[...])
    nt = lambda a, b: lax.dot_general(a, b, (((1,), (1,)), ((), ())), preferred_element_type=F32)
    logits = nt(w1, h1) + (nt(w1, h2) + nt(w2, h1))
    m = jnp.max(logits, axis=0, keepdims=True)
    e = jnp.exp(logits - m)
    aff = e / jnp.sum(e, axis=0, keepdims=True)
    for t in range(ROW_TILE // MOE_TOKEN_TILE):
        aff_out[t] = aff[:, t * MOE_TOKEN_TILE:(t + 1) * MOE_TOKEN_TILE]


def _outproj(x, na, o_f, o_b, z, gdn_norm, w_out, ln2, w_router_t):
    n = x.shape[0]
    tm = ROW_TILE
    sub = tm // MOE_TOKEN_TILE
    row = lambda w: pl.BlockSpec((tm, w), lambda i: (i, 0))
    const = lambda s: pl.BlockSpec(s, lambda i: (0, 0))
    return pl.pallas_call(
        _outproj_kernel, grid=(n // tm,),
        in_specs=[row(D_MODEL), row(NA_WIDTH), row(GDN_WIDTH), row(GDN_WIDTH), row(GDN_WIDTH),
                  const((1, GDN_HEAD_DIM)), const((D_MODEL, D_MODEL)), const((1, D_MODEL)), const((N_EXPERTS, D_MODEL))],
        out_specs=[row(D_MODEL), row(D_MODEL),
                   pl.BlockSpec((sub, N_EXPERTS, MOE_TOKEN_TILE), lambda i: (i, 0, 0))],
        out_shape=[jax.ShapeDtypeStruct((n, D_MODEL), F32), jax.ShapeDtypeStruct((n, D_MODEL), BF16),
                   jax.ShapeDtypeStruct((n // MOE_TOKEN_TILE, N_EXPERTS, MOE_TOKEN_TILE), F32)],
        compiler_params=_cparams(("parallel",)),
        name="outproj_router")(x, na, o_f, o_b, z, gdn_norm, w_out, ln2, w_router_t)


def _select_kernel(cap, aff_ref, lr_out, gate_out, tbl_out):
    aff = aff_ref[...]
    nt = aff.shape[0]
    count = lambda m: jnp.sum(jnp.sum(jnp.where(m, 1.0, 0.0), axis=0), axis=-1, keepdims=True)
    capf = jnp.float32(cap)
    as_f32 = lambda b: pltpu.bitcast(b, F32)

    def bit_step(i, thr):
        cand = jnp.bitwise_or(thr, jnp.left_shift(jnp.int32(1), 30 - i))
        return jnp.where(count(aff >= as_f32(cand)[None]) >= capf, cand, thr)

    thr = lax.fori_loop(0, 31, bit_step, jnp.zeros((N_EXPERTS, 1), jnp.int32))
    lo, hi = as_f32(thr), as_f32(thr + 1)

    def mid_step(_, lh):
        lo, hi = lh
        mid = 0.5 * (lo + hi)
        up = count(aff >= mid[None]) >= capf
        return jnp.where(up, mid, lo), jnp.where(up, hi, mid)

    lo, hi = lax.fori_loop(0, 32, mid_step, (lo, hi))
    gt = aff >= hi[None]
    eq = (aff >= lo[None]) & jnp.logical_not(gt)
    need = capf - count(gt)
    tok = (lax.broadcasted_iota(jnp.int32, aff.shape, 0) * MOE_TOKEN_TILE
           + lax.broadcasted_iota(jnp.int32, aff.shape, 2))

    def tie_step(i, bound):
        cand = bound + jnp.left_shift(jnp.int32(1), 15 - i)
        return jnp.where(count(eq & (tok < cand[None])) <= need, cand, bound)

    bound = lax.fori_loop(0, 16, tie_step, jnp.zeros((N_EXPERTS, 1), jnp.int32))
    mask = gt | (eq & (tok < bound[None]))
    maskf = jnp.where(mask, 1.0, 0.0)
    t_r = lax.broadcasted_iota(jnp.int32, (MOE_TOKEN_TILE, MOE_TOKEN_TILE), 0)
    t_c = lax.broadcasted_iota(jnp.int32, (MOE_TOKEN_TILE, MOE_TOKEN_TILE), 1)
    before = jnp.where(t_r < t_c, 1.0, 0.0).astype(BF16)
    rank = jnp.dot(maskf.astype(BF16).reshape(nt * N_EXPERTS, MOE_TOKEN_TILE), before,
                   preferred_element_type=F32).reshape(aff.shape)
    lr_out[...] = jnp.where(mask, rank + 1.0, 0.0)
    gate_out[...] = jnp.where(mask, aff, 0.0)
    cnt = jnp.sum(maskf, axis=-1, keepdims=True).astype(jnp.int32)
    run = cnt
    s = 1
    while s < nt:
        run = run + jnp.concatenate([jnp.zeros((s,) + run.shape[1:], jnp.int32), run[:nt - s]], axis=0)
        s *= 2
    lane = lax.broadcasted_iota(jnp.int32, tbl_out.shape, 2)
    tbl_out[...] = jnp.where(lane == 0, run - cnt, jnp.where(lane == 1, cnt, 0))


def _select(aff3, cap):
    nt = aff3.shape[0]
    full = lambda w: pl.BlockSpec((nt, N_EXPERTS, w), lambda: (0, 0, 0))
    return pl.pallas_call(
        functools.partial(_select_kernel, cap), in_specs=[full(MOE_TOKEN_TILE)],
        out_specs=[full(MOE_TOKEN_TILE), full(MOE_TOKEN_TILE), full(LANES)],
        out_shape=[jax.ShapeDtypeStruct(aff3.shape, F32), jax.ShapeDtypeStruct(aff3.shape, F32),
                   jax.ShapeDtypeStruct((nt, N_EXPERTS, LANES), jnp.int32)],
        compiler_params=pltpu.CompilerParams(vmem_limit_bytes=VMEM_LIMIT),
        name="moe_select")(aff3)


def _slot_geometry(start_ref, cnt_ref, j, e):
    start, cnt = start_ref[j, e], cnt_ref[j, e]
    shift = jnp.bitwise_and(start, MOE_SLOT_ALIGN - 1)
    return start - shift, shift, cnt, shift + cnt


def _slot_rounds(start_ref, cnt_ref, j):
    rounds = jnp.int32(0)
    for e in range(N_EXPERTS):
        _, _, cnt, total = _slot_geometry(start_ref, cnt_ref, j, e)
        rounds = jnp.maximum(rounds, jnp.where(cnt > 0, (total + MOE_SLOTS - 1) // MOE_SLOTS, 0))
    return rounds


def _slot_one_hot(lr, shifts, rnd, val):
    p1 = (lax.broadcasted_iota(jnp.int32, (MOE_SLOTS, MOE_TOKEN_TILE), 0) + 1 + rnd * MOE_SLOTS).astype(F32)
    rows = []
    for e in range(N_EXPERTS):
        lre = lr[e:e + 1, :]
        pos = jnp.where(lre > 0.0, lre + shifts[e].astype(F32), 0.0)
        v = 1.0 if val is None else val[e:e + 1, :]
        rows.append(jnp.where(pos == p1, v, 0.0))
    return jnp.concatenate(rows, axis=0)


def _dispatch_kernel(cap, start_ref, cnt_ref, h_ref, lr_ref, xe_hbm, stage, tail, sems):
    j = pl.program_id(0)
    nt = pl.num_programs(0)
    buf = lax.rem(j, 2)
    lr = lr_ref[0]
    h = h_ref[...]
    geo = [_slot_geometry(start_ref, cnt_ref, j, e) for e in range(N_EXPERTS)]
    rounds = _slot_rounds(start_ref, cnt_ref, j)

    def copy(jj, e, rnd, b):
        abase = _slot_geometry(start_ref, cnt_ref, jj, e)[0]
        dst = pl.multiple_of(abase + rnd * MOE_SLOTS, MOE_SLOT_ALIGN)
        return pltpu.make_async_copy(stage.at[b, e, pl.ds(0, MOE_SLOTS)], xe_hbm.at[e, pl.ds(dst, MOE_SLOTS)],
                                     sems.at[b, e])

    @pl.when(j == 0)
    def _():
        tail[...] = jnp.zeros_like(tail)
        stage[...] = jnp.zeros_like(stage)
        for e in range(N_EXPERTS):
            pltpu.make_async_copy(stage.at[1, e, pl.ds(0, MOE_SLOTS)], xe_hbm.at[e, pl.ds(cap, MOE_SLOTS)],
                                  sems.at[1, e]).start()
        for e in range(N_EXPERTS):
            pltpu.make_async_copy(stage.at[1, e, pl.ds(0, MOE_SLOTS)], xe_hbm.at[e, pl.ds(cap, MOE_SLOTS)],
                                  sems.at[1, e]).wait()

    def fill(rnd, first):
        oh = _slot_one_hot(lr, [g[1] for g in geo], rnd, None).astype(BF16)
        rows = jnp.dot(oh, h, preferred_element_type=F32)
        for e in range(N_EXPERTS):
            r = rows[e * MOE_SLOTS:(e + 1) * MOE_SLOTS]
            if first:
                r = jnp.concatenate([r[:MOE_SLOT_ALIGN] + tail[e].astype(F32), r[MOE_SLOT_ALIGN:]], axis=0)
            stage[buf, e, 0:MOE_SLOTS, :] = r.astype(BF16)
        for e in range(N_EXPERTS):
            _, _, cnt, total = geo[e]

            @pl.when((cnt > 0) & ((total - 1) // MOE_SLOTS == rnd))
            def _():
                grp = jnp.right_shift(total, MOE_SLOT_SHIFT) - rnd * (MOE_SLOTS // MOE_SLOT_ALIGN)
                tail[e] = stage[buf, e, pl.ds(pl.multiple_of(grp * MOE_SLOT_ALIGN, MOE_SLOT_ALIGN), MOE_SLOT_ALIGN), :]

    def each_active(jj, rnd, fn):
        for e in range(N_EXPERTS):
            _, _, cnt, total = _slot_geometry(start_ref, cnt_ref, jj, e)

            @pl.when((cnt > 0) & (total > rnd * MOE_SLOTS))
            def _():
                fn(e)

    fill(0, True)
    jp = jnp.maximum(j - 1, 0)

    @pl.when((j > 0) & (_slot_rounds(start_ref, cnt_ref, jp) <= 1))
    def _():
        each_active(jp, 0, lambda e: copy(jp, e, 0, 1 - buf).wait())

    each_active(j, 0, lambda e: copy(j, e, 0, buf).start())

    @pl.when(rounds > 1)
    def _():
        each_active(j, 0, lambda e: copy(j, e, 0, buf).wait())

        def more(r, c):
            fill(r, False)
            each_active(j, r, lambda e: copy(j, e, r, buf).start())
            each_active(j, r, lambda e: copy(j, e, r, buf).wait())
            return c

        lax.fori_loop(1, rounds, more, 0)

    @pl.when((j == nt - 1) & (rounds <= 1))
    def _():
        each_active(j, 0, lambda e: copy(j, e, 0, buf).wait())


def _dispatch(start, cnt, h_bf, lr, cap):
    n = h_bf.shape[0]
    nt = n // MOE_TOKEN_TILE
    return pl.pallas_call(
        functools.partial(_dispatch_kernel, cap),
        grid_spec=pltpu.PrefetchScalarGridSpec(
            num_scalar_prefetch=2, grid=(nt,),
            in_specs=[pl.BlockSpec((MOE_TOKEN_TILE, D_MODEL), lambda j, b, c: (j, 0)),
                      pl.BlockSpec((1, N_EXPERTS, MOE_TOKEN_TILE), lambda j, b, c: (j, 0, 0))],
            out_specs=pl.BlockSpec(memory_space=pl.ANY),
            scratch_shapes=[pltpu.VMEM((2, N_EXPERTS, MOE_SLOTS + MOE_SLOT_ALIGN, D_MODEL), BF16),
                            pltpu.VMEM((N_EXPERTS, MOE_SLOT_ALIGN, D_MODEL), BF16),
                            pltpu.SemaphoreType.DMA((2, N_EXPERTS))]),
        out_shape=jax.ShapeDtypeStruct((N_EXPERTS, cap + MOE_SLOTS, D_MODEL), BF16),
        compiler_params=_cparams(("arbitrary",)),
        name="moe_dispatch")(start, cnt, h_bf, lr)


def _ffn_kernel(ntiles, x_ref, wg_ref, wu_ref, wd_ref, y_out):
    @pl.when(pl.program_id(1) < ntiles)
    def _():
        x = x_ref[0]
        a = jnp.dot(x, wg_ref[0], preferred_element_type=F32)
        b = jnp.dot(x, wu_ref[0], preferred_element_type=F32)
        hid = (_silu(a) * b).astype(BF16)
        y_out[0] = jnp.dot(hid, wd_ref[0], preferred_element_type=F32).astype(y_out.dtype)

    @pl.when(pl.program_id(1) == ntiles)
    def _():
        y_out[...] = jnp.zeros_like(y_out)


def _expert_ffn(xe, cap, w_gate, w_up, w_down):
    tm = FFN_ROW_TILE
    ntiles = cap // tm
    wspec = pl.BlockSpec((1, D_MODEL, EXPERT_FF), lambda ei, j: (ei, 0, 0))
    return pl.pallas_call(
        functools.partial(_ffn_kernel, ntiles), grid=(N_EXPERTS, ntiles + 1),
        in_specs=[pl.BlockSpec((1, tm, D_MODEL), lambda ei, j: (ei, jnp.minimum(j, ntiles - 1), 0)), wspec, wspec,
                  pl.BlockSpec((1, EXPERT_FF, D_MODEL), lambda ei, j: (ei, 0, 0))],
        out_specs=pl.BlockSpec((1, tm, D_MODEL), lambda ei, j: (ei, j, 0)),
        out_shape=jax.ShapeDtypeStruct((N_EXPERTS, cap + tm, D_MODEL), BF16),
        compiler_params=_cparams(("parallel", "arbitrary")),
        name="expert_ffn")(xe, w_gate, w_up, w_down)


def _combine_kernel(final, start_ref, cnt_ref, x_ref, lr_ref, gate_ref, ye_hbm, *rest):
    if final:
        ln_ref, o_ref, slab, sems = rest
    else:
        o_ref, slab, sems = rest
    j = pl.program_id(0)
    nt = pl.num_programs(0)
    buf = lax.rem(j, 2)
    extra = 2
    lr = lr_ref[0]
    gate = gate_ref[0]
    g_hi = gate.astype(BF16).astype(F32)
    g_lo = gate - g_hi
    shifts = [_slot_geometry(start_ref, cnt_ref, j, e)[1] for e in range(N_EXPERTS)]
    rounds = _slot_rounds(start_ref, cnt_ref, j)

    def copy(jj, e, rnd, b):
        abase = _slot_geometry(start_ref, cnt_ref, jj, e)[0]
        src = pl.multiple_of(abase + rnd * MOE_SLOTS, MOE_SLOT_ALIGN)
        return pltpu.make_async_copy(ye_hbm.at[e, pl.ds(src, MOE_SLOTS)], slab.at[b, e], sems.at[b, e])

    def each(jj, rnd, fn, fn_idle=None):
        for e in range(N_EXPERTS):
            _, _, cnt, total = _slot_geometry(start_ref, cnt_ref, jj, e)
            active = (cnt > 0) & (total > rnd * MOE_SLOTS)

            @pl.when(active)
            def _():
                fn(e)

            if fn_idle is not None:
                @pl.when(jnp.logical_not(active))
                def _():
                    fn_idle(e)

    def window(rnd, b):
        w_hi = _slot_one_hot(lr, shifts, rnd, g_hi).astype(BF16)
        w_lo = _slot_one_hot(lr, shifts, rnd, g_lo).astype(BF16)

        def idle(e):
            slab[b, e] = jnp.zeros((MOE_SLOTS, D_MODEL), BF16)

        each(j, rnd, lambda e: copy(j, e, rnd, b).wait(), idle)
        ye = slab[b].reshape(N_EXPERTS * MOE_SLOTS, D_MODEL)
        tn = lambda a, c: lax.dot_general(a, c, (((0,), (0,)), ((), ())), preferred_element_type=F32)
        return tn(w_hi, ye) + tn(w_lo, ye)

    @pl.when(j == 0)
    def _():
        each(j, 0, lambda e: copy(j, e, 0, buf).start())

    jn = jnp.minimum(j + 1, nt - 1)

    @pl.when(j + 1 < nt)
    def _():
        each(jn, 0, lambda e: copy(jn, e, 0, 1 - buf).start())

    y = window(0, buf)

    def more(r, acc):
        each(j, r, lambda e: copy(j, e, r, extra).start())
        return acc + window(r, extra)

    y = lax.cond(rounds > 1, lambda: lax.fori_loop(1, rounds, more, y), lambda: y)
    x2 = x_ref[...] + y
    o_ref[...] = _rms(x2, ln_ref[...]) if final else x2


def _combine(base, cnt, x1, lr, gate3, ye, ln_f):
    n = x1.shape[0]
    nt = n // MOE_TOKEN_TILE
    final = ln_f is not None
    tok = pl.BlockSpec((MOE_TOKEN_TILE, D_MODEL), lambda j, b, c: (j, 0))
    sel = pl.BlockSpec((1, N_EXPERTS, MOE_TOKEN_TILE), lambda j, b, c: (j, 0, 0))
    in_specs = [tok, sel, sel, pl.BlockSpec(memory_space=pl.ANY)]
    args = [base, cnt, x1, lr, gate3, ye]
    if final:
        in_specs.append(pl.BlockSpec((1, D_MODEL), lambda j, b, c: (0, 0)))
        args.append(ln_f)
    return pl.pallas_call(
        functools.partial(_combine_kernel, final),
        grid_spec=pltpu.PrefetchScalarGridSpec(
            num_scalar_prefetch=2, grid=(nt,), in_specs=in_specs, out_specs=tok,
            scratch_shapes=[pltpu.VMEM((3, N_EXPERTS, MOE_SLOTS, D_MODEL), BF16),
                            pltpu.SemaphoreType.DMA((3, N_EXPERTS))]),
        out_shape=jax.ShapeDtypeStruct((n, D_MODEL), F32),
        compiler_params=_cparams(("arbitrary",)),
        name="moe_combine_final" if final else "moe_combine")(*args)


def _moe(x1, h_bf, aff3, wg, wu, wd, ln_f):
    n = h_bf.shape[0]
    nt = n // MOE_TOKEN_TILE
    cap = EC_CAPACITY * n // N_EXPERTS
    lr, gate3, tbl = _select(aff3, cap)
    start, cnt = tbl[:, :, 0], tbl[:, :, 1]
    xe = _dispatch(start, cnt, h_bf, lr, cap)
    ye = _expert_ffn(xe, cap, wg, wu, wd)
    return _combine(start, cnt, x1, lr, gate3, ye, ln_f)


def _prep_layer(l, ln1, w_in, conv_w, a_log, dt_bias, gdn_norm, rpb, w_out, ln2, w_router, w_gate, w_up, w_down):
    w_pad = jnp.pad(w_in[l], ((0, 0), (0, IN_WIDTH_PAD - IN_WIDTH))).astype(BF16)
    return dict(
        ln1=ln1[l].reshape(1, D_MODEL), w_pad=w_pad,
        wgt=jnp.transpose(w_in[l][:, GATE_COL0:IN_WIDTH]).astype(BF16),
        conv_w=jnp.pad(conv_w[l], ((0, 8 - CONV_K), (0, 0))),
        a_log=a_log[l], dt_bias=dt_bias[l], gdn_norm=gdn_norm[l].reshape(1, GDN_HEAD_DIM),
        bias_tab=_na_bias_table(rpb[l]), w_out=w_out[l].astype(BF16), ln2=ln2[l].reshape(1, D_MODEL),
        w_router_t=jnp.transpose(w_router[l]),
        wg=w_gate[l].astype(BF16), wu=w_up[l].astype(BF16), wd=w_down[l].astype(BF16))


def _trunk(x3, layers, ln_f):
    batch, seq, _ = x3.shape
    x = x3.reshape(batch * seq, D_MODEL)
    for li, p in enumerate(layers):
        q, k, v, c, z, gate, gate_t = _inproj(x, p["ln1"], p["w_pad"], p["wgt"])
        na = _neighbourhood_attention(q, k, v, p["bias_tab"], batch)
        gq, gk, gv = _gdn_prep(c, p["conv_w"], batch)
        o_f, o_b = _gdn_scan(gq, gk, gv, gate, gate_t, p["a_log"], p["dt_bias"], batch)
        x1, h_bf, aff3 = _outproj(x, na, o_f, o_b, z, p["gdn_norm"], p["w_out"], p["ln2"], p["w_router_t"])
        last = li == len(layers) - 1
        x = _moe(x1, h_bf, aff3, p["wg"], p["wu"], p["wd"], ln_f.reshape(1, D_MODEL) if last else None)
    return x.reshape(batch, seq, D_MODEL)


def kernel(x_prompt, x_sample, ln1, w_in, conv_w, a_log, dt_bias, gdn_norm, rpb, w_out, ln2, w_router, w_gate, w_up, w_down, ln_f):
    layers = [_prep_layer(l, ln1, w_in, conv_w, a_log, dt_bias, gdn_norm, rpb, w_out, ln2, w_router,
                          w_gate, w_up, w_down) for l in range(ln1.shape[0])]
    return (_trunk(x_prompt, layers, ln_f), _trunk(x_sample, layers, ln_f))
```

```python
import functools

import jax
import jax.numpy as jnp
from jax import lax
from jax.experimental import pallas as pl
from jax.experimental.pallas import tpu as pltpu

F32 = jnp.float32
BF16 = jnp.bfloat16

D_MODEL = 1024
GRID_W = 64
NA_HEADS = 8
NA_HEAD_DIM = 64
NA_WIDTH = 512
NA_ROWS = 8
NA_COLS = 16
GDN_HEADS = 4
GDN_HEAD_DIM = 128
GDN_WIDTH = 512
CONV_K = 5
CHUNK = 64
IN_WIDTH = 3600
N_EXPERTS = 16
EXPERT_FF = 1024
EC_CAPACITY = 2
EPS = 1e-6

LANES = 128
IN_WIDTH_PAD = 3712
GATE_COL0 = 3584
NEG_BIG = -1e30
VMEM_LIMIT = 56 * 1024 * 1024

ROW_TILE = 512
NA_ROW_BLOCK = 16
GDN_BLOCK_CHUNKS = 8
FFN_ROW_TILE = 512
MOE_TOKEN_TILE = 256
MOE_SLOTS = 64
MOE_SLOT_SHIFT = 4
MOE_SLOT_ALIGN = 1 << MOE_SLOT_SHIFT


def _cparams(sem):
    return pltpu.CompilerParams(dimension_semantics=sem, vmem_limit_bytes=VMEM_LIMIT)


def _bdot(a, b):
    return jnp.dot(a.astype(BF16), b.astype(BF16), preferred_element_type=F32)


def _bdot_nt(a, b):
    return lax.dot_general(a.astype(BF16), b.astype(BF16), (((1,), (1,)), ((), ())),
                           preferred_element_type=F32)


def _bdot_tn(a, b):
    return lax.dot_general(a.astype(BF16), b.astype(BF16), (((0,), (0,)), ((), ())),
                           preferred_element_type=F32)


def _split3(x):
    x1 = x.astype(BF16)
    r1 = x - x1.astype(F32)
    x2 = r1.astype(BF16)
    r2 = r1 - x2.astype(F32)
    return x1, x2, r2.astype(BF16)


def _silu(x):
    return x * (1.0 / (1.0 + jnp.exp(-x)))


def _sigmoid(x):
    return 1.0 / (1.0 + jnp.exp(-x))


def _softplus(x):
    return jnp.maximum(x, 0.0) + jnp.log(1.0 + jnp.exp(-jnp.abs(x)))


def _rms(x, g):
    ms = jnp.mean(x * x, axis=-1, keepdims=True)
    return (x * lax.rsqrt(ms + EPS)) * g


def _block_diag(a, b):
    za, zb = jnp.zeros_like(a), jnp.zeros_like(b)
    return jnp.concatenate([jnp.concatenate([a, zb], axis=1), jnp.concatenate([za, b], axis=1)], axis=0)


def _inproj_kernel(x_ref, ln_ref, w_ref, wgt_ref, q_out, k_out, v_out, c_out, z_out, g_out, gt_out):
    h = _rms(x_ref[...], ln_ref[...]).astype(BF16)
    q_out[...] = jnp.dot(h, w_ref[:, 0:512], preferred_element_type=F32).astype(BF16)
    k_out[...] = jnp.dot(h, w_ref[:, 512:1024], preferred_element_type=F32).astype(BF16)
    v_out[...] = jnp.dot(h, w_ref[:, 1024:1536], preferred_element_type=F32).astype(BF16)
    c_out[...] = jnp.dot(h, w_ref[:, 1536:3072], preferred_element_type=F32)
    z_out[...] = jnp.dot(h, w_ref[:, 3072:3584], preferred_element_type=F32)
    g_out[...] = jnp.dot(h, w_ref[:, GATE_COL0:IN_WIDTH_PAD], preferred_element_type=F32)
    gt_out[...] = lax.dot_general(wgt_ref[...], h, (((1,), (1,)), ((), ())), preferred_element_type=F32)


def _inproj(x, ln, w_pad, wgt):
    n = x.shape[0]
    tm = ROW_TILE
    row = lambda w: pl.BlockSpec((tm, w), lambda i: (i, 0))
    const = lambda s: pl.BlockSpec(s, lambda i: (0, 0))
    out_shape = [
        jax.ShapeDtypeStruct((n, NA_WIDTH), BF16), jax.ShapeDtypeStruct((n, NA_WIDTH), BF16),
        jax.ShapeDtypeStruct((n, NA_WIDTH), BF16), jax.ShapeDtypeStruct((n, 3 * GDN_WIDTH), F32),
        jax.ShapeDtypeStruct((n, GDN_WIDTH), F32), jax.ShapeDtypeStruct((n, LANES), F32),
        jax.ShapeDtypeStruct((16, n), F32)]
    out_specs = [row(NA_WIDTH), row(NA_WIDTH), row(NA_WIDTH), row(3 * GDN_WIDTH), row(GDN_WIDTH), row(LANES),
                 pl.BlockSpec((16, tm), lambda i: (0, i))]
    return pl.pallas_call(
        _inproj_kernel, grid=(n // tm,),
        in_specs=[row(D_MODEL), const((1, D_MODEL)), const((D_MODEL, IN_WIDTH_PAD)), const((16, D_MODEL))],
        out_specs=out_specs, out_shape=out_shape, compiler_params=_cparams(("parallel",)),
        name="inproj")(x, ln, w_pad, wgt)


def _na_kernel(rows, q_ref, k_ref, v_ref, bias_ref, o_ref):
    i = pl.program_id(2)
    lane = lax.broadcasted_iota(jnp.int32, (GRID_W, LANES), 1)
    low = lane < NA_HEAD_DIM
    scale = NA_HEAD_DIM ** -0.5

    def scores(rr):
        r = i * NA_ROW_BLOCK + rr
        rs = jnp.clip(r - NA_ROWS // 2, 0, rows - NA_ROWS)
        d0 = rs - r + NA_ROWS - 1
        k0 = pl.multiple_of(rs * GRID_W, GRID_W)
        kw = k_ref[0, pl.ds(k0, NA_ROWS * GRID_W), :]
        q = q_ref[0, rr * GRID_W:(rr + 1) * GRID_W, :] * scale
        zq = jnp.zeros_like(q)
        q2 = jnp.concatenate([jnp.where(low, q, zq), jnp.where(low, zq, q)], axis=0)
        bias = jnp.concatenate([bias_ref[0, d0], bias_ref[1, d0]], axis=0)
        return _bdot_nt(q2, kw) + bias, k0

    def softmax(s, k0):
        m = jnp.max(s, axis=-1, keepdims=True)
        p = jnp.exp(s - m)
        return p.astype(BF16), 1.0 / jnp.sum(p, axis=-1, keepdims=True), k0

    def finish(rr, p, inv, k0):
        vw = v_ref[0, pl.ds(k0, NA_ROWS * GRID_W), :]
        o2 = jnp.dot(p, vw, preferred_element_type=F32) * inv
        o_ref[0, rr * GRID_W:(rr + 1) * GRID_W, :] = jnp.where(low, o2[:GRID_W], o2[GRID_W:]).astype(o_ref.dtype)

    s_q = [scores(0), scores(1)]
    p_q = [softmax(*s_q.pop(0))]
    for rr in range(NA_ROW_BLOCK):
        if rr + 2 < NA_ROW_BLOCK:
            s_q.append(scores(rr + 2))
        if s_q:
            p_q.append(softmax(*s_q.pop(0)))
        finish(rr, *p_q.pop(0))


def _na_bias_table(rpb):
    col = jnp.arange(GRID_W)
    cs = jnp.clip(col - NA_COLS // 2, 0, GRID_W - NA_COLS)
    kcol = jnp.arange(GRID_W)
    valid = (kcol[None, :] >= cs[:, None]) & (kcol[None, :] < cs[:, None] + NA_COLS)
    cidx = jnp.clip(kcol[None, :] - col[:, None] + NA_COLS - 1, 0, 2 * NA_COLS - 2)
    band = jnp.where(valid[None, None], rpb[:, :, cidx], NEG_BIG)
    ridx = jnp.arange(NA_ROWS)[:, None] + jnp.arange(NA_ROWS)[None, :]
    tab = band[:, ridx]
    tab = jnp.transpose(tab, (0, 1, 3, 2, 4))
    return tab.reshape(NA_HEADS, NA_ROWS, GRID_W, NA_ROWS * GRID_W).astype(F32)


def _neighbourhood_attention(q, k, v, bias_tab, batch):
    n = q.shape[0]
    seq = n // batch
    rows = seq // GRID_W
    q3, k3, v3 = (t.reshape(batch, seq, NA_WIDTH) for t in (q, k, v))
    tq = NA_ROW_BLOCK * GRID_W
    out = pl.pallas_call(
        functools.partial(_na_kernel, rows),
        grid=(batch, NA_HEADS // 2, rows // NA_ROW_BLOCK),
        in_specs=[pl.BlockSpec((1, tq, LANES), lambda b, hp, i: (b, i, hp)),
                  pl.BlockSpec((1, seq, LANES), lambda b, hp, i: (b, 0, hp)),
                  pl.BlockSpec((1, seq, LANES), lambda b, hp, i: (b, 0, hp)),
                  pl.BlockSpec((2, NA_ROWS, GRID_W, NA_ROWS * GRID_W), lambda b, hp, i: (hp, 0, 0, 0))],
        out_specs=pl.BlockSpec((1, tq, LANES), lambda b, hp, i: (b, i, hp)),
        out_shape=jax.ShapeDtypeStruct((batch, seq, NA_WIDTH), BF16),
        compiler_params=_cparams(("parallel", "parallel", "arbitrary")),
        name="na_attention")(q3, k3, v3, bias_tab)
    return out.reshape(n, NA_WIDTH)


def _gdn_prep_kernel(nblk, c_ref, prev_ref, next_ref, w_ref, q_out, k_out, v_out, xs):
    i = pl.program_id(1)
    tm = c_ref.shape[1]
    pad = CONV_K // 2
    halo = 8
    xs[0:halo, :] = jnp.where(i > 0, prev_ref[0], 0.0)
    xs[halo:halo + tm, :] = c_ref[0]
    xs[halo + tm:, :] = jnp.where(i < nblk - 1, next_ref[0], 0.0)
    outs = (q_out, k_out, v_out)
    for blk in range(3 * GDN_HEADS):
        sl = slice(blk * GDN_HEAD_DIM, (blk + 1) * GDN_HEAD_DIM)
        acc = xs[halo - pad:halo - pad + tm, sl] * w_ref[0:1, sl]
        for j in range(1, CONV_K):
            acc = acc + xs[halo - pad + j:halo - pad + j + tm, sl] * w_ref[j:j + 1, sl]
        c = _silu(acc)
        which, hd = divmod(blk, GDN_HEADS)
        if which < 2:
            c = c * lax.rsqrt(jnp.sum(c * c, axis=-1, keepdims=True) + EPS)
        outs[which][0, :, hd * GDN_HEAD_DIM:(hd + 1) * GDN_HEAD_DIM] = c.astype(outs[which].dtype)


def _gdn_prep(c, conv_w, batch):
    n = c.shape[0]
    seq = n // batch
    tm = ROW_TILE
    nblk = seq // tm
    c3 = c.reshape(batch, seq, 3 * GDN_WIDTH)
    hb = tm // 8
    spec = pl.BlockSpec((1, tm, GDN_WIDTH), lambda b, i: (b, i, 0))
    outs = pl.pallas_call(
        functools.partial(_gdn_prep_kernel, nblk),
        grid=(batch, nblk),
        in_specs=[pl.BlockSpec((1, tm, 3 * GDN_WIDTH), lambda b, i: (b, i, 0)),
                  pl.BlockSpec((1, 8, 3 * GDN_WIDTH), lambda b, i: (b, jnp.maximum(i * hb - 1, 0), 0)),
                  pl.BlockSpec((1, 8, 3 * GDN_WIDTH), lambda b, i: (b, jnp.minimum((i + 1) * hb, seq // 8 - 1), 0)),
                  pl.BlockSpec((8, 3 * GDN_WIDTH), lambda b, i: (0, 0))],
        out_specs=[spec, spec, spec],
        out_shape=[jax.ShapeDtypeStruct((batch, seq, GDN_WIDTH), BF16)] * 3,
        scratch_shapes=[pltpu.VMEM((tm + 16, 3 * GDN_WIDTH), F32)],
        compiler_params=_cparams(("parallel", "parallel")),
        name="gdn_prep")(c3, c3, c3, conv_w)
    return tuple(t.reshape(n, GDN_WIDTH) for t in outs)


def _gdn_scan_kernel(qf, kf, vf, gf, gtf, qb, kb, vb, gb, gtb, arow, drow, acol, dcol, of_ref, ob_ref, s_ref):
    @pl.when(pl.program_id(1) == 0)
    def _():
        s_ref[...] = jnp.zeros_like(s_ref)

    cb = GDN_BLOCK_CHUNKS
    hw = GDN_HEAD_DIM
    npair = GDN_HEADS // 2
    ri = lax.broadcasted_iota(jnp.int32, (CHUNK, LANES), 0)
    lane = lax.broadcasted_iota(jnp.int32, (CHUNK, LANES), 1)
    lo = lane < CHUNK
    cm = jnp.bitwise_and(lane, CHUNK - 1)
    eye = jnp.where(ri == cm, 1.0, 0.0)
    rr = lax.broadcasted_iota(jnp.int32, (CHUNK, CHUNK), 0)
    cc = lax.broadcasted_iota(jnp.int32, (CHUNK, CHUNK), 1)
    neg_a_row, neg_a_col = -jnp.exp(arow[...]), -jnp.exp(acol[...])
    d_row, d_col = drow[...], dcol[...]
    refs = ((qf, kf, vf, gf, gtf, of_ref), (qb, kb, vb, gb, gtb, ob_ref))
    incl = (ri >= cm, ri <= cm)
    strict = (ri > cm, ri < cm)
    tri = (jnp.where(rr >= cc, 1.0, 0.0).astype(BF16), jnp.where(rr <= cc, 1.0, 0.0).astype(BF16))
    tri_t2 = (jnp.where(cm >= ri, 1.0, 0.0).astype(BF16), jnp.where(cm <= ri, 1.0, 0.0).astype(BF16))
    f32dot = lambda a, b: jnp.dot(a, b, preferred_element_type=F32)

    inst = [(d, c) for d in (0, 1) for c in range(cb)]
    pairs = [(d, c, p) for d, c in inst for p in range(npair)]

    st = {}
    for d, c in inst:
        q_ref, k_ref, v_ref, g_ref, gt_ref, _ = refs[d]
        sl = slice(c * CHUNK, (c + 1) * CHUNK)
        graw = g_ref[0, sl, :]
        g1, g2, g3 = _split3(neg_a_row * _softplus(graw + d_row))
        t1, t2, t3 = _split3(neg_a_col * _softplus(gt_ref[0, c] + d_col))
        st[d, c] = dict(
            gc_full=f32dot(tri[d], g1) + f32dot(tri[d], g2) + f32dot(tri[d], g3),
            gct2=f32dot(t1, tri_t2[d]) + f32dot(t2, tri_t2[d]) + f32dot(t3, tri_t2[d]),
            beta_full=_sigmoid(graw),
            q=q_ref[0, sl, :].astype(F32), k=k_ref[0, sl, :].astype(F32), v=v_ref[0, sl, :].astype(F32))

    pk = {}
    for d, c, p in pairs:
        s = st[d, c]
        h0, h1 = 2 * p, 2 * p + 1
        col = lambda full, base: jnp.where(lo, full[:, base + h0:base + h0 + 1], full[:, base + h1:base + h1 + 1])
        gc = col(s["gc_full"], 8 + 4 * d)
        beta = col(s["beta_full"], 4 * d)
        g_row0 = 8 + 4 * d + h0
        gr = jnp.where(lo[0:1], s["gct2"][g_row0:g_row0 + 1, :], s["gct2"][g_row0 + 1:g_row0 + 2, :])
        hs = lambda t, h: t[:, h * hw:(h + 1) * hw]
        q0, q1 = hs(s["q"], h0) * (hw ** -0.5), hs(s["q"], h1) * (hw ** -0.5)
        k0, k1 = hs(s["k"], h0), hs(s["k"], h1)
        lhs = jnp.concatenate([jnp.concatenate([q0, k0], axis=0), jnp.concatenate([q1, k1], axis=0)], axis=1)
        qkk = _bdot_nt(lhs, _block_diag(k0.astype(BF16), k1.astype(BF16)))
        decay = jnp.where(incl[d], jnp.exp(jnp.where(incl[d], gc - gr, 0.0)), 0.0)
        last = CHUNK - 1 if d == 0 else 0
        g_last = gc[last:last + 1, :]
        pk[d, c, p] = dict(qk=qkk[:CHUNK] * decay, pw=jnp.where(strict[d], qkk[CHUNK:] * beta * decay, 0.0),
                           beta=beta, egc=jnp.exp(gc), g_last=g_last, dk=jnp.exp(g_last - gc),
                           q=(q0, q1), k=(k0, k1), v=(hs(s["v"], h0), hs(s["v"], h1)))

    def bd_of(pw):
        z = jnp.zeros_like(pw)
        return jnp.concatenate([jnp.where(lo, pw, z), jnp.where(lo, z, pw)], axis=0)

    for key in pairs:
        e = pk[key]
        e["tinv"] = eye - e["pw"]
        pw = e["pw"].astype(BF16)
        e["pw"] = f32dot(pw, bd_of(pw))
    for level in range(1, 6):
        for key in pairs:
            e = pk[key]
            pw = e["pw"].astype(BF16)
            if level < 5:
                r = f32dot(jnp.concatenate([pw, e["tinv"].astype(BF16)], axis=0), bd_of(pw))
                e["pw"] = r[:CHUNK]
                e["tinv"] = e["tinv"] + r[CHUNK:]
            else:
                e["tinv"] = e["tinv"] + f32dot(e["tinv"].astype(BF16), bd_of(pw))

    for key in pairs:
        e = pk[key]
        halves = lambda t: (t[:, 0:1], t[:, LANES - 1:LANES])
        b, g, dk = halves(e["beta"]), halves(e["egc"]), halves(e["dk"])
        rhs = [jnp.concatenate([e["v"][hh] * b[hh], e["k"][hh] * (b[hh] * g[hh])], axis=1).astype(BF16)
               for hh in range(2)]
        uw = f32dot(e["tinv"].astype(BF16), _block_diag(rhs[0], rhs[1]))
        e["u"] = (uw[:, 0:hw], uw[:, 2 * hw:3 * hw])
        e["w"] = (uw[:, hw:2 * hw], uw[:, 3 * hw:4 * hw])
        e["qg"] = (e["q"][0] * g[0], e["q"][1] * g[1])
        e["kg"] = (e["k"][0] * dk[0], e["k"][1] * dk[1])
        e["sdec"] = (jnp.exp(e["g_last"][:, 0:1]), jnp.exp(e["g_last"][:, LANES - 1:LANES]))

    for j in range(cb):
        step = [(0, j, p) for p in range(npair)] + [(1, cb - 1 - j, p) for p in range(npair)]
        s_old, wsqs, v_new = {}, {}, {}
        for d, c, p in step:
            e = pk[d, c, p]
            for hh in range(2):
                s_old[d, p, hh] = s_ref[d, 2 * p + hh]
                wsqs[d, p, hh] = _bdot(jnp.concatenate([e["w"][hh], e["qg"][hh]], axis=0), s_old[d, p, hh])
        for d, c, p in step:
            for hh in range(2):
                v_new[d, p, hh] = pk[d, c, p]["u"][hh] - wsqs[d, p, hh][:CHUNK]
        for d, c, p in step:
            e = pk[d, c, p]
            bdv = _block_diag(v_new[d, p, 0].astype(BF16), v_new[d, p, 1].astype(BF16))
            o_pair = f32dot(e["qk"].astype(BF16), bdv)
            o_pair = o_pair + jnp.concatenate([wsqs[d, p, 0][CHUNK:], wsqs[d, p, 1][CHUNK:]], axis=1)
            refs[d][5][0, c * CHUNK:(c + 1) * CHUNK, 2 * p * hw:(2 * p + 2) * hw] = o_pair
            for hh in range(2):
                s_ref[d, 2 * p + hh] = s_old[d, p, hh] * e["sdec"][hh] + _bdot_tn(e["kg"][hh], v_new[d, p, hh])


def _gdn_scan(q, k, v, gate, gate_t, a_log, dt_bias, batch):
    n = q.shape[0]
    seq = n // batch
    tb = GDN_BLOCK_CHUNKS * CHUNK
    nblk = seq // tb
    q3, k3, v3 = (t.reshape(batch, seq, GDN_WIDTH) for t in (q, k, v))
    g3 = gate.reshape(batch, seq, LANES)
    gt4 = jnp.transpose(gate_t.reshape(16, batch, seq // CHUNK, CHUNK), (1, 2, 0, 3))
    arow = jnp.zeros((1, LANES), F32).at[0, 8:16].set(a_log.reshape(-1))
    drow = jnp.zeros((1, LANES), F32).at[0, 8:16].set(dt_bias.reshape(-1))
    acol = arow[0, :16].reshape(16, 1)
    dcol = drow[0, :16].reshape(16, 1)
    fwd = lambda w: pl.BlockSpec((1, tb, w), lambda b, i: (b, i, 0))
    bwd = lambda w: pl.BlockSpec((1, tb, w), lambda b, i: (b, nblk - 1 - i, 0))
    gtf = pl.BlockSpec((1, GDN_BLOCK_CHUNKS, 16, CHUNK), lambda b, i: (b, i, 0, 0))
    gtb = pl.BlockSpec((1, GDN_BLOCK_CHUNKS, 16, CHUNK), lambda b, i: (b, nblk - 1 - i, 0, 0))
    const = lambda s: pl.BlockSpec(s, lambda b, i: (0, 0))
    o_f, o_b = pl.pallas_call(
        _gdn_scan_kernel, grid=(batch, nblk),
        in_specs=[fwd(GDN_WIDTH), fwd(GDN_WIDTH), fwd(GDN_WIDTH), fwd(LANES), gtf,
                  bwd(GDN_WIDTH), bwd(GDN_WIDTH), bwd(GDN_WIDTH), bwd(LANES), gtb,
                  const((1, LANES)), const((1, LANES)), const((16, 1)), const((16, 1))],
        out_specs=[fwd(GDN_WIDTH), bwd(GDN_WIDTH)],
        out_shape=[jax.ShapeDtypeStruct((batch, seq, GDN_WIDTH), F32)] * 2,
        scratch_shapes=[pltpu.VMEM((2, GDN_HEADS, GDN_HEAD_DIM, GDN_HEAD_DIM), F32)],
        compiler_params=_cparams(("parallel", "arbitrary")),
        name="gdn_scan")(q3, k3, v3, g3, gt4, q3, k3, v3, g3, gt4, arow, drow, acol, dcol)
    return o_f.reshape(n, GDN_WIDTH), o_b.reshape(n, GDN_WIDTH)


def _outproj_kernel(x_ref, na_ref, of_ref, ob_ref, z_ref, gn_ref, w_ref, ln_ref, wr_ref, x1_out, h_out, aff_out):
    o = of_ref[...] + ob_ref[...]
    z = z_ref[...]
    gn = gn_ref[...]
    acc = jnp.dot(na_ref[...], w_ref[0:NA_WIDTH, :], preferred_element_type=F32)
    for hd in range(GDN_HEADS):
        sl = slice(hd * GDN_HEAD_DIM, (hd + 1) * GDN_HEAD_DIM)
        oh = o[:, sl]
        oh = oh * lax.rsqrt(jnp.mean(oh * oh, axis=-1, keepdims=True) + EPS) * gn
        oh = oh * _silu(z[:, sl])
        acc = acc + jnp.dot(oh.astype(BF16), w_ref[NA_WIDTH + hd * GDN_HEAD_DIM:NA_WIDTH + (hd + 1) * GDN_HEAD_DIM, :],
                            preferred_element_type=F32)
    x1 = x_ref[...] + acc
    x1_out[...] = x1
    h = _rms(x1, ln_ref[...])
    h_out[...] = h.astype(BF16)
    h1, h2, _ = _split3(h)
    w1, w2, _ = _split3(wr_ref[...])
    nt = lambda a, b: lax.dot_general(a, b, (((1,), (1,)), ((), ())), preferred_element_type=F32)
    logits = nt(w1, h1) + (nt(w1, h2) + nt(w2, h1))
    m = jnp.max(logits, axis=0, keepdims=True)
    e = jnp.exp(logits - m)
    aff = e / jnp.sum(e, axis=0, keepdims=True)
    for t in range(ROW_TILE // MOE_TOKEN_TILE):
        aff_out[t] = aff[:, t * MOE_TOKEN_TILE:(t + 1) * MOE_TOKEN_TILE]


def _outproj(x, na, o_f, o_b, z, gdn_norm, w_out, ln2, w_router_t):
    n = x.shape[0]
    tm = ROW_TILE
    sub = tm // MOE_TOKEN_TILE
    row = lambda w: pl.BlockSpec((tm, w), lambda i: (i, 0))
    const = lambda s: pl.BlockSpec(s, lambda i: (0, 0))
    return pl.pallas_call(
        _outproj_kernel, grid=(n // tm,),
        in_specs=[row(D_MODEL), row(NA_WIDTH), row(GDN_WIDTH), row(GDN_WIDTH), row(GDN_WIDTH),
                  const((1, GDN_HEAD_DIM)), const((D_MODEL, D_MODEL)), const((1, D_MODEL)), const((N_EXPERTS, D_MODEL))],
        out_specs=[row(D_MODEL), row(D_MODEL),
                   pl.BlockSpec((sub, N_EXPERTS, MOE_TOKEN_TILE), lambda i: (i, 0, 0))],
        out_shape=[jax.ShapeDtypeStruct((n, D_MODEL), F32), jax.ShapeDtypeStruct((n, D_MODEL), BF16),
                   jax.ShapeDtypeStruct((n // MOE_TOKEN_TILE, N_EXPERTS, MOE_TOKEN_TILE), F32)],
        compiler_params=_cparams(("parallel",)),
        name="outproj_router")(x, na, o_f, o_b, z, gdn_norm, w_out, ln2, w_router_t)


def _select_kernel(cap, aff_ref, lr_out, gate_out, tbl_out):
    aff = aff_ref[...]
    nt = aff.shape[0]
    count = lambda m: jnp.sum(jnp.sum(jnp.where(m, 1.0, 0.0), axis=0), axis=-1, keepdims=True)
    capf = jnp.float32(cap)
    as_f32 = lambda b: pltpu.bitcast(b, F32)

    def bit_step(i, thr):
        cand = jnp.bitwise_or(thr, jnp.left_shift(jnp.int32(1), 30 - i))
        return jnp.where(count(aff >= as_f32(cand)[None]) >= capf, cand, thr)

    thr = lax.fori_loop(0, 31, bit_step, jnp.zeros((N_EXPERTS, 1), jnp.int32))
    lo, hi = as_f32(thr), as_f32(thr + 1)

    def mid_step(_, lh):
        lo, hi = lh
        mid = 0.5 * (lo + hi)
        up = count(aff >= mid[None]) >= capf
        return jnp.where(up, mid, lo), jnp.where(up, hi, mid)

    lo, hi = lax.fori_loop(0, 32, mid_step, (lo, hi))
    gt = aff >= hi[None]
    eq = (aff >= lo[None]) & jnp.logical_not(gt)
    need = capf - count(gt)
    tok = (lax.broadcasted_iota(jnp.int32, aff.shape, 0) * MOE_TOKEN_TILE
           + lax.broadcasted_iota(jnp.int32, aff.shape, 2))

    def tie_step(i, bound):
        cand = bound + jnp.left_shift(jnp.int32(1), 15 - i)
        return jnp.where(count(eq & (tok < cand[None])) <= need, cand, bound)

    bound = lax.fori_loop(0, 16, tie_step, jnp.zeros((N_EXPERTS, 1), jnp.int32))
    mask = gt | (eq & (tok < bound[None]))
    maskf = jnp.where(mask, 1.0, 0.0)
    t_r = lax.broadcasted_iota(jnp.int32, (MOE_TOKEN_TILE, MOE_TOKEN_TILE), 0)
    t_c = lax.broadcasted_iota(jnp.int32, (MOE_TOKEN_TILE, MOE_TOKEN_TILE), 1)
    before = jnp.where(t_r < t_c, 1.0, 0.0).astype(BF16)
    rank = jnp.dot(maskf.astype(BF16).reshape(nt * N_EXPERTS, MOE_TOKEN_TILE), before,
                   preferred_element_type=F32).reshape(aff.shape)
    lr_out[...] = jnp.where(mask, rank + 1.0, 0.0)
    gate_out[...] = jnp.where(mask, aff, 0.0)
    cnt = jnp.sum(maskf, axis=-1, keepdims=True).astype(jnp.int32)
    run = cnt
    s = 1
    while s < nt:
        run = run + jnp.concatenate([jnp.zeros((s,) + run.shape[1:], jnp.int32), run[:nt - s]], axis=0)
        s *= 2
    lane = lax.broadcasted_iota(jnp.int32, tbl_out.shape, 2)
    tbl_out[...] = jnp.where(lane == 0, run - cnt, jnp.where(lane == 1, cnt, 0))


def _select(aff3, cap):
    nt = aff3.shape[0]
    full = lambda w: pl.BlockSpec((nt, N_EXPERTS, w), lambda: (0, 0, 0))
    return pl.pallas_call(
        functools.partial(_select_kernel, cap), in_specs=[full(MOE_TOKEN_TILE)],
        out_specs=[full(MOE_TOKEN_TILE), full(MOE_TOKEN_TILE), full(LANES)],
        out_shape=[jax.ShapeDtypeStruct(aff3.shape, F32), jax.ShapeDtypeStruct(aff3.shape, F32),
                   jax.ShapeDtypeStruct((nt, N_EXPERTS, LANES), jnp.int32)],
        compiler_params=pltpu.CompilerParams(vmem_limit_bytes=VMEM_LIMIT),
        name="moe_select")(aff3)


def _slot_geometry(start_ref, cnt_ref, j, e):
    start, cnt = start_ref[j, e], cnt_ref[j, e]
    shift = jnp.bitwise_and(start, MOE_SLOT_ALIGN - 1)
    return start - shift, shift, cnt, shift + cnt


def _slot_rounds(start_ref, cnt_ref, j):
    rounds = jnp.int32(0)
    for e in range(N_EXPERTS):
        _, _, cnt, total = _slot_geometry(start_ref, cnt_ref, j, e)
        rounds = jnp.maximum(rounds, jnp.where(cnt > 0, (total + MOE_SLOTS - 1) // MOE_SLOTS, 0))
    return rounds


def _slot_one_hot(lr, shifts, rnd, val):
    p1 = (lax.broadcasted_iota(jnp.int32, (MOE_SLOTS, MOE_TOKEN_TILE), 0) + 1 + rnd * MOE_SLOTS).astype(F32)
    rows = []
    for e in range(N_EXPERTS):
        lre = lr[e:e + 1, :]
        pos = jnp.where(lre > 0.0, lre + shifts[e].astype(F32), 0.0)
        v = 1.0 if val is None else val[e:e + 1, :]
        rows.append(jnp.where(pos == p1, v, 0.0))
    return jnp.concatenate(rows, axis=0)


def _dispatch_kernel(cap, start_ref, cnt_ref, h_ref, lr_ref, xe_hbm, stage, tail, sems):
    j = pl.program_id(0)
    nt = pl.num_programs(0)
    buf = lax.rem(j, 2)
    lr = lr_ref[0]
    h = h_ref[...]
    geo = [_slot_geometry(start_ref, cnt_ref, j, e) for e in range(N_EXPERTS)]
    rounds = _slot_rounds(start_ref, cnt_ref, j)

    def copy(jj, e, rnd, b):
        abase = _slot_geometry(start_ref, cnt_ref, jj, e)[0]
        dst = pl.multiple_of(abase + rnd * MOE_SLOTS, MOE_SLOT_ALIGN)
        return pltpu.make_async_copy(stage.at[b, e, pl.ds(0, MOE_SLOTS)], xe_hbm.at[e, pl.ds(dst, MOE_SLOTS)],
                                     sems.at[b, e])

    @pl.when(j == 0)
    def _():
        tail[...] = jnp.zeros_like(tail)
        stage[...] = jnp.zeros_like(stage)
        for e in range(N_EXPERTS):
            pltpu.make_async_copy(stage.at[1, e, pl.ds(0, MOE_SLOTS)], xe_hbm.at[e, pl.ds(cap, MOE_SLOTS)],
                                  sems.at[1, e]).start()
        for e in range(N_EXPERTS):
            pltpu.make_async_copy(stage.at[1, e, pl.ds(0, MOE_SLOTS)], xe_hbm.at[e, pl.ds(cap, MOE_SLOTS)],
                                  sems.at[1, e]).wait()

    def fill(rnd, first):
        oh = _slot_one_hot(lr, [g[1] for g in geo], rnd, None).astype(BF16)
        rows = jnp.dot(oh, h, preferred_element_type=F32)
        for e in range(N_EXPERTS):
            r = rows[e * MOE_SLOTS:(e + 1) * MOE_SLOTS]
            if first:
                r = jnp.concatenate([r[:MOE_SLOT_ALIGN] + tail[e].astype(F32), r[MOE_SLOT_ALIGN:]], axis=0)
            stage[buf, e, 0:MOE_SLOTS, :] = r.astype(BF16)
        for e in range(N_EXPERTS):
            _, _, cnt, total = geo[e]

            @pl.when((cnt > 0) & ((total - 1) // MOE_SLOTS == rnd))
            def _():
                grp = jnp.right_shift(total, MOE_SLOT_SHIFT) - rnd * (MOE_SLOTS // MOE_SLOT_ALIGN)
                tail[e] = stage[buf, e, pl.ds(pl.multiple_of(grp * MOE_SLOT_ALIGN, MOE_SLOT_ALIGN), MOE_SLOT_ALIGN), :]

    def each_active(jj, rnd, fn):
        for e in range(N_EXPERTS):
            _, _, cnt, total = _slot_geometry(start_ref, cnt_ref, jj, e)

            @pl.when((cnt > 0) & (total > rnd * MOE_SLOTS))
            def _():
                fn(e)

    fill(0, True)
    jp = jnp.maximum(j - 1, 0)

    @pl.when((j > 0) & (_slot_rounds(start_ref, cnt_ref, jp) <= 1))
    def _():
        each_active(jp, 0, lambda e: copy(jp, e, 0, 1 - buf).wait())

    each_active(j, 0, lambda e: copy(j, e, 0, buf).start())

    @pl.when(rounds > 1)
    def _():
        each_active(j, 0, lambda e: copy(j, e, 0, buf).wait())

        def more(r, c):
            fill(r, False)
            each_active(j, r, lambda e: copy(j, e, r, buf).start())
            each_active(j, r, lambda e: copy(j, e, r, buf).wait())
            return c

        lax.fori_loop(1, rounds, more, 0)

    @pl.when((j == nt - 1) & (rounds <= 1))
    def _():
        each_active(j, 0, lambda e: copy(j, e, 0, buf).wait())


def _dispatch(start, cnt, h_bf, lr, cap):
    n = h_bf.shape[0]
    nt = n // MOE_TOKEN_TILE
    return pl.pallas_call(
        functools.partial(_dispatch_kernel, cap),
        grid_spec=pltpu.PrefetchScalarGridSpec(
            num_scalar_prefetch=2, grid=(nt,),
            in_specs=[pl.BlockSpec((MOE_TOKEN_TILE, D_MODEL), lambda j, b, c: (j, 0)),
                      pl.BlockSpec((1, N_EXPERTS, MOE_TOKEN_TILE), lambda j, b, c: (j, 0, 0))],
            out_specs=pl.BlockSpec(memory_space=pl.ANY),
            scratch_shapes=[pltpu.VMEM((2, N_EXPERTS, MOE_SLOTS + MOE_SLOT_ALIGN, D_MODEL), BF16),
                            pltpu.VMEM((N_EXPERTS, MOE_SLOT_ALIGN, D_MODEL), BF16),
                            pltpu.SemaphoreType.DMA((2, N_EXPERTS))]),
        out_shape=jax.ShapeDtypeStruct((N_EXPERTS, cap + MOE_SLOTS, D_MODEL), BF16),
        compiler_params=_cparams(("arbitrary",)),
        name="moe_dispatch")(start, cnt, h_bf, lr)


def _ffn_kernel(ntiles, x_ref, wg_ref, wu_ref, wd_ref, y_out, wg_s, wu_s, wd_s):
    @pl.when(pl.program_id(1) == 0)
    def _():
        wg_s[...] = wg_ref[0, 0].astype(BF16)
        wu_s[...] = wu_ref[0, 0].astype(BF16)
        wd_s[...] = wd_ref[0, 0].astype(BF16)

    @pl.when(pl.program_id(1) < ntiles)
    def _():
        x = x_ref[0]
        a = jnp.dot(x, wg_s[...], preferred_element_type=F32)
        b = jnp.dot(x, wu_s[...], preferred_element_type=F32)
        hid = (_silu(a) * b).astype(BF16)
        y_out[0] = jnp.dot(hid, wd_s[...], preferred_element_type=F32).astype(y_out.dtype)

    @pl.when(pl.program_id(1) == ntiles)
    def _():
        y_out[...] = jnp.zeros_like(y_out)


def _expert_ffn(xe, cap, layer, w_gate, w_up, w_down):
    tm = FFN_ROW_TILE
    ntiles = cap // tm
    wspec = pl.BlockSpec((1, 1, D_MODEL, EXPERT_FF), lambda ei, j: (layer, ei, 0, 0))
    return pl.pallas_call(
        functools.partial(_ffn_kernel, ntiles), grid=(N_EXPERTS, ntiles + 1),
        in_specs=[pl.BlockSpec((1, tm, D_MODEL), lambda ei, j: (ei, jnp.minimum(j, ntiles - 1), 0)), wspec, wspec,
                  pl.BlockSpec((1, 1, EXPERT_FF, D_MODEL), lambda ei, j: (layer, ei, 0, 0))],
        out_specs=pl.BlockSpec((1, tm, D_MODEL), lambda ei, j: (ei, j, 0)),
        out_shape=jax.ShapeDtypeStruct((N_EXPERTS, cap + tm, D_MODEL), BF16),
        scratch_shapes=[pltpu.VMEM((D_MODEL, EXPERT_FF), BF16), pltpu.VMEM((D_MODEL, EXPERT_FF), BF16),
                        pltpu.VMEM((EXPERT_FF, D_MODEL), BF16)],
        compiler_params=_cparams(("parallel", "arbitrary")),
        name="expert_ffn")(xe, w_gate, w_up, w_down)


def _combine_kernel(final, start_ref, cnt_ref, x_ref, lr_ref, gate_ref, ye_hbm, *rest):
    if final:
        ln_ref, o_ref, slab, sems = rest
    else:
        o_ref, slab, sems = rest
    j = pl.program_id(0)
    nt = pl.num_programs(0)
    buf = lax.rem(j, 2)
    extra = 2
    lr = lr_ref[0]
    gate = gate_ref[0]
    g_hi = gate.astype(BF16).astype(F32)
    g_lo = gate - g_hi
    shifts = [_slot_geometry(start_ref, cnt_ref, j, e)[1] for e in range(N_EXPERTS)]
    rounds = _slot_rounds(start_ref, cnt_ref, j)

    def copy(jj, e, rnd, b):
        abase = _slot_geometry(start_ref, cnt_ref, jj, e)[0]
        src = pl.multiple_of(abase + rnd * MOE_SLOTS, MOE_SLOT_ALIGN)
        return pltpu.make_async_copy(ye_hbm.at[e, pl.ds(src, MOE_SLOTS)], slab.at[b, e], sems.at[b, e])

    def each(jj, rnd, fn, fn_idle=None):
        for e in range(N_EXPERTS):
            _, _, cnt, total = _slot_geometry(start_ref, cnt_ref, jj, e)
            active = (cnt > 0) & (total > rnd * MOE_SLOTS)

            @pl.when(active)
            def _():
                fn(e)

            if fn_idle is not None:
                @pl.when(jnp.logical_not(active))
                def _():
                    fn_idle(e)

    def window(rnd, b):
        w_hi = _slot_one_hot(lr, shifts, rnd, g_hi).astype(BF16)
        w_lo = _slot_one_hot(lr, shifts, rnd, g_lo).astype(BF16)

        def idle(e):
            slab[b, e] = jnp.zeros((MOE_SLOTS, D_MODEL), BF16)

        each(j, rnd, lambda e: copy(j, e, rnd, b).wait(), idle)
        ye = slab[b].reshape(N_EXPERTS * MOE_SLOTS, D_MODEL)
        tn = lambda a, c: lax.dot_general(a, c, (((0,), (0,)), ((), ())), preferred_element_type=F32)
        return tn(w_hi, ye) + tn(w_lo, ye)

    @pl.when(j == 0)
    def _():
        each(j, 0, lambda e: copy(j, e, 0, buf).start())

    jn = jnp.minimum(j + 1, nt - 1)

    @pl.when(j + 1 < nt)
    def _():
        each(jn, 0, lambda e: copy(jn, e, 0, 1 - buf).start())

    y = window(0, buf)

    def more(r, acc):
        each(j, r, lambda e: copy(j, e, r, extra).start())
        return acc + window(r, extra)

    y = lax.cond(rounds > 1, lambda: lax.fori_loop(1, rounds, more, y), lambda: y)
    x2 = x_ref[...] + y
    o_ref[...] = _rms(x2, ln_ref[...]) if final else x2


def _combine(base, cnt, x1, lr, gate3, ye, ln_f):
    n = x1.shape[0]
    nt = n // MOE_TOKEN_TILE
    final = ln_f is not None
    tok = pl.BlockSpec((MOE_TOKEN_TILE, D_MODEL), lambda j, b, c: (j, 0))
    sel = pl.BlockSpec((1, N_EXPERTS, MOE_TOKEN_TILE), lambda j, b, c: (j, 0, 0))
    in_specs = [tok, sel, sel, pl.BlockSpec(memory_space=pl.ANY)]
    args = [base, cnt, x1, lr, gate3, ye]
    if final:
        in_specs.append(pl.BlockSpec((1, D_MODEL), lambda j, b, c: (0, 0)))
        args.append(ln_f)
    return pl.pallas_call(
        functools.partial(_combine_kernel, final),
        grid_spec=pltpu.PrefetchScalarGridSpec(
            num_scalar_prefetch=2, grid=(nt,), in_specs=in_specs, out_specs=tok,
            scratch_shapes=[pltpu.VMEM((3, N_EXPERTS, MOE_SLOTS, D_MODEL), BF16),
                            pltpu.SemaphoreType.DMA((3, N_EXPERTS))]),
        out_shape=jax.ShapeDtypeStruct((n, D_MODEL), F32),
        compiler_params=_cparams(("arbitrary",)),
        name="moe_combine_final" if final else "moe_combine")(*args)


def _moe(x1, h_bf, aff3, layer, w_gate, w_up, w_down, ln_f):
    n = h_bf.shape[0]
    cap = EC_CAPACITY * n // N_EXPERTS
    lr, gate3, tbl = _select(aff3, cap)
    start, cnt = tbl[:, :, 0], tbl[:, :, 1]
    xe = _dispatch(start, cnt, h_bf, lr, cap)
    ye = _expert_ffn(xe, cap, layer, w_gate, w_up, w_down)
    return _combine(start, cnt, x1, lr, gate3, ye, ln_f)


def _prep_layer(l, ln1, w_in, conv_w, a_log, dt_bias, gdn_norm, rpb, w_out, ln2, w_router):
    w_pad = jnp.pad(w_in[l], ((0, 0), (0, IN_WIDTH_PAD - IN_WIDTH))).astype(BF16)
    return dict(
        ln1=ln1[l].reshape(1, D_MODEL), w_pad=w_pad,
        wgt=jnp.transpose(w_in[l][:, GATE_COL0:IN_WIDTH]).astype(BF16),
        conv_w=jnp.pad(conv_w[l], ((0, 8 - CONV_K), (0, 0))),
        a_log=a_log[l], dt_bias=dt_bias[l], gdn_norm=gdn_norm[l].reshape(1, GDN_HEAD_DIM),
        bias_tab=_na_bias_table(rpb[l]), w_out=w_out[l].astype(BF16), ln2=ln2[l].reshape(1, D_MODEL),
        w_router_t=jnp.transpose(w_router[l]))


def _trunk(x3, layers, experts, ln_f):
    batch, seq, _ = x3.shape
    x = x3.reshape(batch * seq, D_MODEL)
    for li, p in enumerate(layers):
        q, k, v, c, z, gate, gate_t = _inproj(x, p["ln1"], p["w_pad"], p["wgt"])
        na = _neighbourhood_attention(q, k, v, p["bias_tab"], batch)
        gq, gk, gv = _gdn_prep(c, p["conv_w"], batch)
        o_f, o_b = _gdn_scan(gq, gk, gv, gate, gate_t, p["a_log"], p["dt_bias"], batch)
        x1, h_bf, aff3 = _outproj(x, na, o_f, o_b, z, p["gdn_norm"], p["w_out"], p["ln2"], p["w_router_t"])
        last = li == len(layers) - 1
        x = _moe(x1, h_bf, aff3, li, *experts, ln_f.reshape(1, D_MODEL) if last else None)
    return x.reshape(batch, seq, D_MODEL)


def kernel(x_prompt, x_sample, ln1, w_in, conv_w, a_log, dt_bias, gdn_norm, rpb, w_out, ln2, w_router, w_gate, w_up, w_down, ln_f):
    layers = [_prep_layer(l, ln1, w_in, conv_w, a_log, dt_bias, gdn_norm, rpb, w_out, ln2, w_router)
              for l in range(ln1.shape[0])]
    experts = (w_gate, w_up, w_down)
    return (_trunk(x_prompt, layers, experts, ln_f), _trunk(x_sample, layers, experts, ln_f))
```

```python
import functools

import jax
import jax.numpy as jnp
from jax import lax
from jax.experimental import pallas as pl
from jax.experimental.pallas import tpu as pltpu

F32 = jnp.float32
BF16 = jnp.bfloat16

D_MODEL = 1024
GRID_W = 64
NA_HEADS = 8
NA_HEAD_DIM = 64
NA_WIDTH = 512
NA_ROWS = 8
NA_COLS = 16
GDN_HEADS = 4
GDN_HEAD_DIM = 128
GDN_WIDTH = 512
CONV_K = 5
CHUNK = 64
IN_WIDTH = 3600
N_EXPERTS = 16
EXPERT_FF = 1024
EC_CAPACITY = 2
EPS = 1e-6

LANES = 128
IN_WIDTH_PAD = 3712
GATE_COL0 = 3584
NEG_BIG = -1e30
VMEM_LIMIT = 56 * 1024 * 1024

ROW_TILE = 512
NA_ROW_BLOCK = 32
GDN_BLOCK_CHUNKS = 8
FFN_ROW_TILE = 512
MOE_TOKEN_TILE = 256
MOE_SLOTS = 64
MOE_SLOT_SHIFT = 4
MOE_SLOT_ALIGN = 1 << MOE_SLOT_SHIFT


def _cparams(sem):
    return pltpu.CompilerParams(dimension_semantics=sem, vmem_limit_bytes=VMEM_LIMIT)


def _bdot(a, b):
    return jnp.dot(a.astype(BF16), b.astype(BF16), preferred_element_type=F32)


def _bdot_nt(a, b):
    return lax.dot_general(a.astype(BF16), b.astype(BF16), (((1,), (1,)), ((), ())),
                           preferred_element_type=F32)


def _bdot_tn(a, b):
    return lax.dot_general(a.astype(BF16), b.astype(BF16), (((0,), (0,)), ((), ())),
                           preferred_element_type=F32)


def _split3(x):
    x1 = x.astype(BF16)
    r1 = x - x1.astype(F32)
    x2 = r1.astype(BF16)
    r2 = r1 - x2.astype(F32)
    return x1, x2, r2.astype(BF16)


def _silu(x):
    return x * (1.0 / (1.0 + jnp.exp(-x)))


def _sigmoid(x):
    return 1.0 / (1.0 + jnp.exp(-x))


def _softplus(x):
    return jnp.maximum(x, 0.0) + jnp.log(1.0 + jnp.exp(-jnp.abs(x)))


def _rms(x, g):
    ms = jnp.mean(x * x, axis=-1, keepdims=True)
    return (x * lax.rsqrt(ms + EPS)) * g


def _block_diag(a, b):
    za, zb = jnp.zeros_like(a), jnp.zeros_like(b)
    return jnp.concatenate([jnp.concatenate([a, zb], axis=1), jnp.concatenate([za, b], axis=1)], axis=0)


def _inproj_kernel(x_ref, ln_ref, w_ref, wgt_ref, q_out, k_out, v_out, c_out, z_out, g_out, gt_out):
    h = _rms(x_ref[...], ln_ref[...]).astype(BF16)
    q_out[...] = jnp.dot(h, w_ref[:, 0:512], preferred_element_type=F32).astype(BF16)
    k_out[...] = jnp.dot(h, w_ref[:, 512:1024], preferred_element_type=F32).astype(BF16)
    v_out[...] = jnp.dot(h, w_ref[:, 1024:1536], preferred_element_type=F32).astype(BF16)
    c_out[...] = jnp.dot(h, w_ref[:, 1536:3072], preferred_element_type=F32)
    z_out[...] = jnp.dot(h, w_ref[:, 3072:3584], preferred_element_type=F32)
    g_out[...] = jnp.dot(h, w_ref[:, GATE_COL0:IN_WIDTH_PAD], preferred_element_type=F32)
    gt_out[...] = lax.dot_general(wgt_ref[...], h, (((1,), (1,)), ((), ())), preferred_element_type=F32)


def _inproj(x, ln, w_pad, wgt):
    n = x.shape[0]
    tm = ROW_TILE
    row = lambda w: pl.BlockSpec((tm, w), lambda i: (i, 0))
    const = lambda s: pl.BlockSpec(s, lambda i: (0, 0))
    out_shape = [
        jax.ShapeDtypeStruct((n, NA_WIDTH), BF16), jax.ShapeDtypeStruct((n, NA_WIDTH), BF16),
        jax.ShapeDtypeStruct((n, NA_WIDTH), BF16), jax.ShapeDtypeStruct((n, 3 * GDN_WIDTH), F32),
        jax.ShapeDtypeStruct((n, GDN_WIDTH), F32), jax.ShapeDtypeStruct((n, LANES), F32),
        jax.ShapeDtypeStruct((16, n), F32)]
    out_specs = [row(NA_WIDTH), row(NA_WIDTH), row(NA_WIDTH), row(3 * GDN_WIDTH), row(GDN_WIDTH), row(LANES),
                 pl.BlockSpec((16, tm), lambda i: (0, i))]
    return pl.pallas_call(
        _inproj_kernel, grid=(n // tm,),
        in_specs=[row(D_MODEL), const((1, D_MODEL)), const((D_MODEL, IN_WIDTH_PAD)), const((16, D_MODEL))],
        out_specs=out_specs, out_shape=out_shape, compiler_params=_cparams(("parallel",)),
        name="inproj")(x, ln, w_pad, wgt)


def _na_kernel(rows, q_ref, k_ref, v_ref, bias_ref, o_ref):
    i = pl.program_id(2)
    lane = lax.broadcasted_iota(jnp.int32, (GRID_W, LANES), 1)
    low = lane < NA_HEAD_DIM
    scale = NA_HEAD_DIM ** -0.5

    def scores(rr):
        r = i * NA_ROW_BLOCK + rr
        rs = jnp.clip(r - NA_ROWS // 2, 0, rows - NA_ROWS)
        d0 = rs - r + NA_ROWS - 1
        k0 = pl.multiple_of(rs * GRID_W, GRID_W)
        kw = k_ref[0, pl.ds(k0, NA_ROWS * GRID_W), :]
        q = q_ref[0, rr * GRID_W:(rr + 1) * GRID_W, :] * scale
        zq = jnp.zeros_like(q)
        q2 = jnp.concatenate([jnp.where(low, q, zq), jnp.where(low, zq, q)], axis=0)
        bias = jnp.concatenate([bias_ref[0, d0], bias_ref[1, d0]], axis=0)
        return _bdot_nt(q2, kw) + bias, k0

    def softmax(s, k0):
        m = jnp.max(s, axis=-1, keepdims=True)
        p = jnp.exp(s - m)
        return p.astype(BF16), 1.0 / jnp.sum(p, axis=-1, keepdims=True), k0

    def finish(rr, p, inv, k0):
        vw = v_ref[0, pl.ds(k0, NA_ROWS * GRID_W), :]
        o2 = jnp.dot(p, vw, preferred_element_type=F32) * inv
        o_ref[0, rr * GRID_W:(rr + 1) * GRID_W, :] = jnp.where(low, o2[:GRID_W], o2[GRID_W:]).astype(o_ref.dtype)

    s_q = [scores(0), scores(1)]
    p_q = [softmax(*s_q.pop(0))]
    for rr in range(NA_ROW_BLOCK):
        if rr + 2 < NA_ROW_BLOCK:
            s_q.append(scores(rr + 2))
        if s_q:
            p_q.append(softmax(*s_q.pop(0)))
        finish(rr, *p_q.pop(0))


def _na_bias_table(rpb):
    col = jnp.arange(GRID_W)
    cs = jnp.clip(col - NA_COLS // 2, 0, GRID_W - NA_COLS)
    kcol = jnp.arange(GRID_W)
    valid = (kcol[None, :] >= cs[:, None]) & (kcol[None, :] < cs[:, None] + NA_COLS)
    cidx = jnp.clip(kcol[None, :] - col[:, None] + NA_COLS - 1, 0, 2 * NA_COLS - 2)
    band = jnp.where(valid[None, None], rpb[:, :, cidx], NEG_BIG)
    ridx = jnp.arange(NA_ROWS)[:, None] + jnp.arange(NA_ROWS)[None, :]
    tab = band[:, ridx]
    tab = jnp.transpose(tab, (0, 1, 3, 2, 4))
    return tab.reshape(NA_HEADS, NA_ROWS, GRID_W, NA_ROWS * GRID_W).astype(F32)


def _neighbourhood_attention(q, k, v, bias_tab, batch):
    n = q.shape[0]
    seq = n // batch
    rows = seq // GRID_W
    q3, k3, v3 = (t.reshape(batch, seq, NA_WIDTH) for t in (q, k, v))
    assert rows % NA_ROW_BLOCK == 0, (rows, NA_ROW_BLOCK)
    tq = NA_ROW_BLOCK * GRID_W
    out = pl.pallas_call(
        functools.partial(_na_kernel, rows),
        grid=(batch, NA_HEADS // 2, rows // NA_ROW_BLOCK),
        in_specs=[pl.BlockSpec((1, tq, LANES), lambda b, hp, i: (b, i, hp)),
                  pl.BlockSpec((1, seq, LANES), lambda b, hp, i: (b, 0, hp)),
                  pl.BlockSpec((1, seq, LANES), lambda b, hp, i: (b, 0, hp)),
                  pl.BlockSpec((2, NA_ROWS, GRID_W, NA_ROWS * GRID_W), lambda b, hp, i: (hp, 0, 0, 0))],
        out_specs=pl.BlockSpec((1, tq, LANES), lambda b, hp, i: (b, i, hp)),
        out_shape=jax.ShapeDtypeStruct((batch, seq, NA_WIDTH), BF16),
        compiler_params=_cparams(("parallel", "parallel", "arbitrary")),
        name="na_attention")(q3, k3, v3, bias_tab)
    return out.reshape(n, NA_WIDTH)


def _gdn_prep_kernel(nblk, c_ref, prev_ref, next_ref, w_ref, q_out, k_out, v_out, xs):
    i = pl.program_id(1)
    tm = c_ref.shape[1]
    pad = CONV_K // 2
    halo = 8
    xs[0:halo, :] = jnp.where(i > 0, prev_ref[0], 0.0)
    xs[halo:halo + tm, :] = c_ref[0]
    xs[halo + tm:, :] = jnp.where(i < nblk - 1, next_ref[0], 0.0)
    outs = (q_out, k_out, v_out)
    for blk in range(3 * GDN_HEADS):
        sl = slice(blk * GDN_HEAD_DIM, (blk + 1) * GDN_HEAD_DIM)
        acc = xs[halo - pad:halo - pad + tm, sl] * w_ref[0:1, sl]
        for j in range(1, CONV_K):
            acc = acc + xs[halo - pad + j:halo - pad + j + tm, sl] * w_ref[j:j + 1, sl]
        c = _silu(acc)
        which, hd = divmod(blk, GDN_HEADS)
        if which < 2:
            c = c * lax.rsqrt(jnp.sum(c * c, axis=-1, keepdims=True) + EPS)
        outs[which][0, :, hd * GDN_HEAD_DIM:(hd + 1) * GDN_HEAD_DIM] = c.astype(outs[which].dtype)


def _gdn_prep(c, conv_w, batch):
    n = c.shape[0]
    seq = n // batch
    tm = ROW_TILE
    nblk = seq // tm
    c3 = c.reshape(batch, seq, 3 * GDN_WIDTH)
    hb = tm // 8
    spec = pl.BlockSpec((1, tm, GDN_WIDTH), lambda b, i: (b, i, 0))
    outs = pl.pallas_call(
        functools.partial(_gdn_prep_kernel, nblk),
        grid=(batch, nblk),
        in_specs=[pl.BlockSpec((1, tm, 3 * GDN_WIDTH), lambda b, i: (b, i, 0)),
                  pl.BlockSpec((1, 8, 3 * GDN_WIDTH), lambda b, i: (b, jnp.maximum(i * hb - 1, 0), 0)),
                  pl.BlockSpec((1, 8, 3 * GDN_WIDTH), lambda b, i: (b, jnp.minimum((i + 1) * hb, seq // 8 - 1), 0)),
                  pl.BlockSpec((8, 3 * GDN_WIDTH), lambda b, i: (0, 0))],
        out_specs=[spec, spec, spec],
        out_shape=[jax.ShapeDtypeStruct((batch, seq, GDN_WIDTH), BF16)] * 3,
        scratch_shapes=[pltpu.VMEM((tm + 16, 3 * GDN_WIDTH), F32)],
        compiler_params=_cparams(("parallel", "parallel")),
        name="gdn_prep")(c3, c3, c3, conv_w)
    return tuple(t.reshape(n, GDN_WIDTH) for t in outs)


def _gdn_scan_kernel(qf, kf, vf, gf, gtf, qb, kb, vb, gb, gtb, arow, drow, acol, dcol, of_ref, ob_ref, s_ref):
    @pl.when(pl.program_id(1) == 0)
    def _():
        s_ref[...] = jnp.zeros_like(s_ref)

    cb = GDN_BLOCK_CHUNKS
    hw = GDN_HEAD_DIM
    npair = GDN_HEADS // 2
    ri = lax.broadcasted_iota(jnp.int32, (CHUNK, LANES), 0)
    lane = lax.broadcasted_iota(jnp.int32, (CHUNK, LANES), 1)
    lo = lane < CHUNK
    cm = jnp.bitwise_and(lane, CHUNK - 1)
    eye = jnp.where(ri == cm, 1.0, 0.0)
    rr = lax.broadcasted_iota(jnp.int32, (CHUNK, CHUNK), 0)
    cc = lax.broadcasted_iota(jnp.int32, (CHUNK, CHUNK), 1)
    neg_a_row, neg_a_col = -jnp.exp(arow[...]), -jnp.exp(acol[...])
    d_row, d_col = drow[...], dcol[...]
    refs = ((qf, kf, vf, gf, gtf, of_ref), (qb, kb, vb, gb, gtb, ob_ref))
    incl = (ri >= cm, ri <= cm)
    strict = (ri > cm, ri < cm)
    tri = (jnp.where(rr >= cc, 1.0, 0.0).astype(BF16), jnp.where(rr <= cc, 1.0, 0.0).astype(BF16))
    tri_t2 = (jnp.where(cm >= ri, 1.0, 0.0).astype(BF16), jnp.where(cm <= ri, 1.0, 0.0).astype(BF16))
    f32dot = lambda a, b: jnp.dot(a, b, preferred_element_type=F32)

    inst = [(d, c) for d in (0, 1) for c in range(cb)]
    pairs = [(d, c, p) for d, c in inst for p in range(npair)]

    st = {}
    for d, c in inst:
        q_ref, k_ref, v_ref, g_ref, gt_ref, _ = refs[d]
        sl = slice(c * CHUNK, (c + 1) * CHUNK)
        graw = g_ref[0, sl, :]
        g1, g2, g3 = _split3(neg_a_row * _softplus(graw + d_row))
        t1, t2, t3 = _split3(neg_a_col * _softplus(gt_ref[0, c] + d_col))
        st[d, c] = dict(
            gc_full=f32dot(tri[d], g1) + f32dot(tri[d], g2) + f32dot(tri[d], g3),
            gct2=f32dot(t1, tri_t2[d]) + f32dot(t2, tri_t2[d]) + f32dot(t3, tri_t2[d]),
            beta_full=_sigmoid(graw),
            q=q_ref[0, sl, :].astype(F32), k=k_ref[0, sl, :].astype(F32), v=v_ref[0, sl, :].astype(F32))

    pk = {}
    for d, c, p in pairs:
        s = st[d, c]
        h0, h1 = 2 * p, 2 * p + 1
        col = lambda full, base: jnp.where(lo, full[:, base + h0:base + h0 + 1], full[:, base + h1:base + h1 + 1])
        gc = col(s["gc_full"], 8 + 4 * d)
        beta = col(s["beta_full"], 4 * d)
        g_row0 = 8 + 4 * d + h0
        gr = jnp.where(lo[0:1], s["gct2"][g_row0:g_row0 + 1, :], s["gct2"][g_row0 + 1:g_row0 + 2, :])
        hs = lambda t, h: t[:, h * hw:(h + 1) * hw]
        q0, q1 = hs(s["q"], h0) * (hw ** -0.5), hs(s["q"], h1) * (hw ** -0.5)
        k0, k1 = hs(s["k"], h0), hs(s["k"], h1)
        lhs = jnp.concatenate([jnp.concatenate([q0, k0], axis=0), jnp.concatenate([q1, k1], axis=0)], axis=1)
        qkk = _bdot_nt(lhs, _block_diag(k0.astype(BF16), k1.astype(BF16)))
        decay = jnp.where(incl[d], jnp.exp(jnp.where(incl[d], gc - gr, 0.0)), 0.0)
        last = CHUNK - 1 if d == 0 else 0
        g_last = gc[last:last + 1, :]
        pk[d, c, p] = dict(qk=qkk[:CHUNK] * decay, pw=jnp.where(strict[d], qkk[CHUNK:] * beta * decay, 0.0),
                           beta=beta, egc=jnp.exp(gc), g_last=g_last, dk=jnp.exp(g_last - gc),
                           q=(q0, q1), k=(k0, k1), v=(hs(s["v"], h0), hs(s["v"], h1)))

    def bd_of(pw):
        z = jnp.zeros_like(pw)
        return jnp.concatenate([jnp.where(lo, pw, z), jnp.where(lo, z, pw)], axis=0)

    for key in pairs:
        e = pk[key]
        e["tinv"] = eye - e["pw"]
        pw = e["pw"].astype(BF16)
        e["pw"] = f32dot(pw, bd_of(pw))
    for level in range(1, 6):
        for key in pairs:
            e = pk[key]
            pw = e["pw"].astype(BF16)
            if level < 5:
                r = f32dot(jnp.concatenate([pw, e["tinv"].astype(BF16)], axis=0), bd_of(pw))
                e["pw"] = r[:CHUNK]
                e["tinv"] = e["tinv"] + r[CHUNK:]
            else:
                e["tinv"] = e["tinv"] + f32dot(e["tinv"].astype(BF16), bd_of(pw))

    for key in pairs:
        e = pk[key]
        halves = lambda t: (t[:, 0:1], t[:, LANES - 1:LANES])
        b, g, dk = halves(e["beta"]), halves(e["egc"]), halves(e["dk"])
        rhs = [jnp.concatenate([e["v"][hh] * b[hh], e["k"][hh] * (b[hh] * g[hh])], axis=1).astype(BF16)
               for hh in range(2)]
        uw = f32dot(e["tinv"].astype(BF16), _block_diag(rhs[0], rhs[1]))
        e["u"] = (uw[:, 0:hw], uw[:, 2 * hw:3 * hw])
        e["w"] = (uw[:, hw:2 * hw], uw[:, 3 * hw:4 * hw])
        e["qg"] = (e["q"][0] * g[0], e["q"][1] * g[1])
        e["kg"] = (e["k"][0] * dk[0], e["k"][1] * dk[1])
        e["sdec"] = (jnp.exp(e["g_last"][:, 0:1]), jnp.exp(e["g_last"][:, LANES - 1:LANES]))

    for j in range(cb):
        step = [(0, j, p) for p in range(npair)] + [(1, cb - 1 - j, p) for p in range(npair)]
        s_old, wsqs, v_new = {}, {}, {}
        for d, c, p in step:
            e = pk[d, c, p]
            for hh in range(2):
                s_old[d, p, hh] = s_ref[d, 2 * p + hh]
                wsqs[d, p, hh] = _bdot(jnp.concatenate([e["w"][hh], e["qg"][hh]], axis=0), s_old[d, p, hh])
        for d, c, p in step:
            for hh in range(2):
                v_new[d, p, hh] = pk[d, c, p]["u"][hh] - wsqs[d, p, hh][:CHUNK]
        for d, c, p in step:
            e = pk[d, c, p]
            bdv = _block_diag(v_new[d, p, 0].astype(BF16), v_new[d, p, 1].astype(BF16))
            o_pair = f32dot(e["qk"].astype(BF16), bdv)
            o_pair = o_pair + jnp.concatenate([wsqs[d, p, 0][CHUNK:], wsqs[d, p, 1][CHUNK:]], axis=1)
            refs[d][5][0, c * CHUNK:(c + 1) * CHUNK, 2 * p * hw:(2 * p + 2) * hw] = o_pair
            for hh in range(2):
                s_ref[d, 2 * p + hh] = s_old[d, p, hh] * e["sdec"][hh] + _bdot_tn(e["kg"][hh], v_new[d, p, hh])


def _gdn_scan(q, k, v, gate, gate_t, a_log, dt_bias, batch):
    n = q.shape[0]
    seq = n // batch
    tb = GDN_BLOCK_CHUNKS * CHUNK
    nblk = seq // tb
    q3, k3, v3 = (t.reshape(batch, seq, GDN_WIDTH) for t in (q, k, v))
    g3 = gate.reshape(batch, seq, LANES)
    gt4 = jnp.transpose(gate_t.reshape(16, batch, seq // CHUNK, CHUNK), (1, 2, 0, 3))
    arow = jnp.zeros((1, LANES), F32).at[0, 8:16].set(a_log.reshape(-1))
    drow = jnp.zeros((1, LANES), F32).at[0, 8:16].set(dt_bias.reshape(-1))
    acol = arow[0, :16].reshape(16, 1)
    dcol = drow[0, :16].reshape(16, 1)
    fwd = lambda w: pl.BlockSpec((1, tb, w), lambda b, i: (b, i, 0))
    bwd = lambda w: pl.BlockSpec((1, tb, w), lambda b, i: (b, nblk - 1 - i, 0))
    gtf = pl.BlockSpec((1, GDN_BLOCK_CHUNKS, 16, CHUNK), lambda b, i: (b, i, 0, 0))
    gtb = pl.BlockSpec((1, GDN_BLOCK_CHUNKS, 16, CHUNK), lambda b, i: (b, nblk - 1 - i, 0, 0))
    const = lambda s: pl.BlockSpec(s, lambda b, i: (0, 0))
    o_f, o_b = pl.pallas_call(
        _gdn_scan_kernel, grid=(batch, nblk),
        in_specs=[fwd(GDN_WIDTH), fwd(GDN_WIDTH), fwd(GDN_WIDTH), fwd(LANES), gtf,
                  bwd(GDN_WIDTH), bwd(GDN_WIDTH), bwd(GDN_WIDTH), bwd(LANES), gtb,
                  const((1, LANES)), const((1, LANES)), const((16, 1)), const((16, 1))],
        out_specs=[fwd(GDN_WIDTH), bwd(GDN_WIDTH)],
        out_shape=[jax.ShapeDtypeStruct((batch, seq, GDN_WIDTH), F32)] * 2,
        scratch_shapes=[pltpu.VMEM((2, GDN_HEADS, GDN_HEAD_DIM, GDN_HEAD_DIM), F32)],
        compiler_params=_cparams(("parallel", "arbitrary")),
        name="gdn_scan")(q3, k3, v3, g3, gt4, q3, k3, v3, g3, gt4, arow, drow, acol, dcol)
    return o_f.reshape(n, GDN_WIDTH), o_b.reshape(n, GDN_WIDTH)


def _outproj_kernel(x_ref, na_ref, of_ref, ob_ref, z_ref, gn_ref, w_ref, ln_ref, wr_ref, x1_out, h_out, aff_out):
    o = of_ref[...] + ob_ref[...]
    z = z_ref[...]
    gn = gn_ref[...]
    acc = jnp.dot(na_ref[...], w_ref[0:NA_WIDTH, :], preferred_element_type=F32)
    for hd in range(GDN_HEADS):
        sl = slice(hd * GDN_HEAD_DIM, (hd + 1) * GDN_HEAD_DIM)
        oh = o[:, sl]
        oh = oh * lax.rsqrt(jnp.mean(oh * oh, axis=-1, keepdims=True) + EPS) * gn
        oh = oh * _silu(z[:, sl])
        acc = acc + jnp.dot(oh.astype(BF16), w_ref[NA_WIDTH + hd * GDN_HEAD_DIM:NA_WIDTH + (hd + 1) * GDN_HEAD_DIM, :],
                            preferred_element_type=F32)
    x1 = x_ref[...] + acc
    x1_out[...] = x1
    h = _rms(x1, ln_ref[...])
    h_out[...] = h.astype(BF16)
    h1, h2, _ = _split3(h)
    w1, w2, _ = _split3(wr_ref[...])
    nt = lambda a, b: lax.dot_general(a, b, (((1,), (1,)), ((), ())), preferred_element_type=F32)
    logits = nt(w1, h1) + (nt(w1, h2) + nt(w2, h1))
    m = jnp.max(logits, axis=0, keepdims=True)
    e = jnp.exp(logits - m)
    aff = e / jnp.sum(e, axis=0, keepdims=True)
    for t in range(ROW_TILE // MOE_TOKEN_TILE):
        aff_out[t] = aff[:, t * MOE_TOKEN_TILE:(t + 1) * MOE_TOKEN_TILE]


def _outproj(x, na, o_f, o_b, z, gdn_norm, w_out, ln2, w_router_t):
    n = x.shape[0]
    tm = ROW_TILE
    sub = tm // MOE_TOKEN_TILE
    row = lambda w: pl.BlockSpec((tm, w), lambda i: (i, 0))
    const = lambda s: pl.BlockSpec(s, lambda i: (0, 0))
    return pl.pallas_call(
        _outproj_kernel, grid=(n // tm,),
        in_specs=[row(D_MODEL), row(NA_WIDTH), row(GDN_WIDTH), row(GDN_WIDTH), row(GDN_WIDTH),
                  const((1, GDN_HEAD_DIM)), const((D_MODEL, D_MODEL)), const((1, D_MODEL)), const((N_EXPERTS, D_MODEL))],
        out_specs=[row(D_MODEL), row(D_MODEL),
                   pl.BlockSpec((sub, N_EXPERTS, MOE_TOKEN_TILE), lambda i: (i, 0, 0))],
        out_shape=[jax.ShapeDtypeStruct((n, D_MODEL), F32), jax.ShapeDtypeStruct((n, D_MODEL), BF16),
                   jax.ShapeDtypeStruct((n // MOE_TOKEN_TILE, N_EXPERTS, MOE_TOKEN_TILE), F32)],
        compiler_params=_cparams(("parallel",)),
        name="outproj_router")(x, na, o_f, o_b, z, gdn_norm, w_out, ln2, w_router_t)


def _select_kernel(cap, aff_ref, lr_out, gate_out, tbl_out):
    aff = aff_ref[...]
    nt = aff.shape[0]
    count = lambda m: jnp.sum(jnp.sum(jnp.where(m, 1.0, 0.0), axis=0), axis=-1, keepdims=True)
    capf = jnp.float32(cap)
    as_f32 = lambda b: pltpu.bitcast(b, F32)

    def bit_step(i, thr):
        cand = jnp.bitwise_or(thr, jnp.left_shift(jnp.int32(1), 30 - i))
        return jnp.where(count(aff >= as_f32(cand)[None]) >= capf, cand, thr)

    thr = lax.fori_loop(0, 31, bit_step, jnp.zeros((N_EXPERTS, 1), jnp.int32))
    lo, hi = as_f32(thr), as_f32(thr + 1)

    def mid_step(_, lh):
        lo, hi = lh
        mid = 0.5 * (lo + hi)
        up = count(aff >= mid[None]) >= capf
        return jnp.where(up, mid, lo), jnp.where(up, hi, mid)

    lo, hi = lax.fori_loop(0, 32, mid_step, (lo, hi))
    gt = aff >= hi[None]
    eq = (aff >= lo[None]) & jnp.logical_not(gt)
    need = capf - count(gt)
    tok = (lax.broadcasted_iota(jnp.int32, aff.shape, 0) * MOE_TOKEN_TILE
           + lax.broadcasted_iota(jnp.int32, aff.shape, 2))

    def tie_step(i, bound):
        cand = bound + jnp.left_shift(jnp.int32(1), 15 - i)
        return jnp.where(count(eq & (tok < cand[None])) <= need, cand, bound)

    bound = lax.fori_loop(0, 16, tie_step, jnp.zeros((N_EXPERTS, 1), jnp.int32))
    mask = gt | (eq & (tok < bound[None]))
    maskf = jnp.where(mask, 1.0, 0.0)
    t_r = lax.broadcasted_iota(jnp.int32, (MOE_TOKEN_TILE, MOE_TOKEN_TILE), 0)
    t_c = lax.broadcasted_iota(jnp.int32, (MOE_TOKEN_TILE, MOE_TOKEN_TILE), 1)
    before = jnp.where(t_r < t_c, 1.0, 0.0).astype(BF16)
    rank = jnp.dot(maskf.astype(BF16).reshape(nt * N_EXPERTS, MOE_TOKEN_TILE), before,
                   preferred_element_type=F32).reshape(aff.shape)
    lr_out[...] = jnp.where(mask, rank + 1.0, 0.0)
    gate_out[...] = jnp.where(mask, aff, 0.0)
    cnt = jnp.sum(maskf, axis=-1, keepdims=True).astype(jnp.int32)
    run = cnt
    s = 1
    while s < nt:
        run = run + jnp.concatenate([jnp.zeros((s,) + run.shape[1:], jnp.int32), run[:nt - s]], axis=0)
        s *= 2
    lane = lax.broadcasted_iota(jnp.int32, tbl_out.shape, 2)
    tbl_out[...] = jnp.where(lane == 0, run - cnt, jnp.where(lane == 1, cnt, 0))


def _select(aff3, cap):
    nt = aff3.shape[0]
    full = lambda w: pl.BlockSpec((nt, N_EXPERTS, w), lambda: (0, 0, 0))
    return pl.pallas_call(
        functools.partial(_select_kernel, cap), in_specs=[full(MOE_TOKEN_TILE)],
        out_specs=[full(MOE_TOKEN_TILE), full(MOE_TOKEN_TILE), full(LANES)],
        out_shape=[jax.ShapeDtypeStruct(aff3.shape, F32), jax.ShapeDtypeStruct(aff3.shape, F32),
                   jax.ShapeDtypeStruct((nt, N_EXPERTS, LANES), jnp.int32)],
        compiler_params=pltpu.CompilerParams(vmem_limit_bytes=VMEM_LIMIT),
        name="moe_select")(aff3)


def _slot_geometry(start_ref, cnt_ref, j, e):
    start, cnt = start_ref[j, e], cnt_ref[j, e]
    shift = jnp.bitwise_and(start, MOE_SLOT_ALIGN - 1)
    return start - shift, shift, cnt, shift + cnt


def _slot_rounds(start_ref, cnt_ref, j):
    rounds = jnp.int32(0)
    for e in range(N_EXPERTS):
        _, _, cnt, total = _slot_geometry(start_ref, cnt_ref, j, e)
        rounds = jnp.maximum(rounds, jnp.where(cnt > 0, (total + MOE_SLOTS - 1) // MOE_SLOTS, 0))
    return rounds


def _slot_one_hot(lr, shifts, rnd, val):
    p1 = (lax.broadcasted_iota(jnp.int32, (MOE_SLOTS, MOE_TOKEN_TILE), 0) + 1 + rnd * MOE_SLOTS).astype(F32)
    rows = []
    for e in range(N_EXPERTS):
        lre = lr[e:e + 1, :]
        pos = jnp.where(lre > 0.0, lre + shifts[e].astype(F32), 0.0)
        v = 1.0 if val is None else val[e:e + 1, :]
        rows.append(jnp.where(pos == p1, v, 0.0))
    return jnp.concatenate(rows, axis=0)


def _dispatch_kernel(cap, start_ref, cnt_ref, h_ref, lr_ref, xe_hbm, stage, tail, sems):
    j = pl.program_id(0)
    nt = pl.num_programs(0)
    buf = lax.rem(j, 2)
    lr = lr_ref[0]
    h = h_ref[...]
    geo = [_slot_geometry(start_ref, cnt_ref, j, e) for e in range(N_EXPERTS)]
    rounds = _slot_rounds(start_ref, cnt_ref, j)

    def copy(jj, e, rnd, b):
        abase = _slot_geometry(start_ref, cnt_ref, jj, e)[0]
        dst = pl.multiple_of(abase + rnd * MOE_SLOTS, MOE_SLOT_ALIGN)
        return pltpu.make_async_copy(stage.at[b, e, pl.ds(0, MOE_SLOTS)], xe_hbm.at[e, pl.ds(dst, MOE_SLOTS)],
                                     sems.at[b, e])

    @pl.when(j == 0)
    def _():
        tail[...] = jnp.zeros_like(tail)
        stage[...] = jnp.zeros_like(stage)
        for e in range(N_EXPERTS):
            pltpu.make_async_copy(stage.at[1, e, pl.ds(0, MOE_SLOTS)], xe_hbm.at[e, pl.ds(cap, MOE_SLOTS)],
                                  sems.at[1, e]).start()
        for e in range(N_EXPERTS):
            pltpu.make_async_copy(stage.at[1, e, pl.ds(0, MOE_SLOTS)], xe_hbm.at[e, pl.ds(cap, MOE_SLOTS)],
                                  sems.at[1, e]).wait()

    def fill(rnd, first):
        oh = _slot_one_hot(lr, [g[1] for g in geo], rnd, None).astype(BF16)
        rows = jnp.dot(oh, h, preferred_element_type=F32)
        for e in range(N_EXPERTS):
            r = rows[e * MOE_SLOTS:(e + 1) * MOE_SLOTS]
            if first:
                r = jnp.concatenate([r[:MOE_SLOT_ALIGN] + tail[e].astype(F32), r[MOE_SLOT_ALIGN:]], axis=0)
            stage[buf, e, 0:MOE_SLOTS, :] = r.astype(BF16)
        for e in range(N_EXPERTS):
            _, _, cnt, total = geo[e]

            @pl.when((cnt > 0) & ((total - 1) // MOE_SLOTS == rnd))
            def _():
                grp = jnp.right_shift(total, MOE_SLOT_SHIFT) - rnd * (MOE_SLOTS // MOE_SLOT_ALIGN)
                tail[e] = stage[buf, e, pl.ds(pl.multiple_of(grp * MOE_SLOT_ALIGN, MOE_SLOT_ALIGN), MOE_SLOT_ALIGN), :]

    def each_active(jj, rnd, fn):
        for e in range(N_EXPERTS):
            _, _, cnt, total = _slot_geometry(start_ref, cnt_ref, jj, e)

            @pl.when((cnt > 0) & (total > rnd * MOE_SLOTS))
            def _():
                fn(e)

    fill(0, True)
    jp = jnp.maximum(j - 1, 0)

    @pl.when((j > 0) & (_slot_rounds(start_ref, cnt_ref, jp) <= 1))
    def _():
        each_active(jp, 0, lambda e: copy(jp, e, 0, 1 - buf).wait())

    each_active(j, 0, lambda e: copy(j, e, 0, buf).start())

    @pl.when(rounds > 1)
    def _():
        each_active(j, 0, lambda e: copy(j, e, 0, buf).wait())

        def more(r, c):
            fill(r, False)
            each_active(j, r, lambda e: copy(j, e, r, buf).start())
            each_active(j, r, lambda e: copy(j, e, r, buf).wait())
            return c

        lax.fori_loop(1, rounds, more, 0)

    @pl.when((j == nt - 1) & (rounds <= 1))
    def _():
        each_active(j, 0, lambda e: copy(j, e, 0, buf).wait())


def _dispatch(start, cnt, h_bf, lr, cap):
    n = h_bf.shape[0]
    nt = n // MOE_TOKEN_TILE
    return pl.pallas_call(
        functools.partial(_dispatch_kernel, cap),
        grid_spec=pltpu.PrefetchScalarGridSpec(
            num_scalar_prefetch=2, grid=(nt,),
            in_specs=[pl.BlockSpec((MOE_TOKEN_TILE, D_MODEL), lambda j, b, c: (j, 0)),
                      pl.BlockSpec((1, N_EXPERTS, MOE_TOKEN_TILE), lambda j, b, c: (j, 0, 0))],
            out_specs=pl.BlockSpec(memory_space=pl.ANY),
            scratch_shapes=[pltpu.VMEM((2, N_EXPERTS, MOE_SLOTS + MOE_SLOT_ALIGN, D_MODEL), BF16),
                            pltpu.VMEM((N_EXPERTS, MOE_SLOT_ALIGN, D_MODEL), BF16),
                            pltpu.SemaphoreType.DMA((2, N_EXPERTS))]),
        out_shape=jax.ShapeDtypeStruct((N_EXPERTS, cap + MOE_SLOTS, D_MODEL), BF16),
        compiler_params=_cparams(("arbitrary",)),
        name="moe_dispatch")(start, cnt, h_bf, lr)


def _ffn_kernel(x_ref, wg_ref, wu_ref, wd_ref, y_out, wg_s, wu_s, wd_s):
    @pl.when(pl.program_id(1) == 0)
    def _():
        wg_s[...] = wg_ref[0, 0].astype(BF16)
        wu_s[...] = wu_ref[0, 0].astype(BF16)
        wd_s[...] = wd_ref[0, 0].astype(BF16)

        y_out[...] = jnp.zeros_like(y_out)

    @pl.when(pl.program_id(1) > 0)
    def _():
        x = x_ref[0]
        a = jnp.dot(x, wg_s[...], preferred_element_type=F32)
        b = jnp.dot(x, wu_s[...], preferred_element_type=F32)
        hid = (_silu(a) * b).astype(BF16)
        y_out[0] = jnp.dot(hid, wd_s[...], preferred_element_type=F32).astype(y_out.dtype)


def _expert_ffn(xe, cap, layer, w_gate, w_up, w_down):
    tm = FFN_ROW_TILE
    ntiles = cap // tm
    wspec = pl.BlockSpec((1, 1, D_MODEL, EXPERT_FF), lambda ei, j: (layer, ei, 0, 0))
    return pl.pallas_call(
        _ffn_kernel, grid=(N_EXPERTS, ntiles + 1),
        in_specs=[pl.BlockSpec((1, tm, D_MODEL), lambda ei, j: (ei, jnp.maximum(j - 1, 0), 0)), wspec, wspec,
                  pl.BlockSpec((1, 1, EXPERT_FF, D_MODEL), lambda ei, j: (layer, ei, 0, 0))],
        out_specs=pl.BlockSpec((1, tm, D_MODEL), lambda ei, j: (ei, jnp.where(j == 0, ntiles, j - 1), 0)),
        out_shape=jax.ShapeDtypeStruct((N_EXPERTS, cap + tm, D_MODEL), BF16),
        scratch_shapes=[pltpu.VMEM((D_MODEL, EXPERT_FF), BF16), pltpu.VMEM((D_MODEL, EXPERT_FF), BF16),
                        pltpu.VMEM((EXPERT_FF, D_MODEL), BF16)],
        compiler_params=_cparams(("parallel", "arbitrary")),
        name="expert_ffn")(xe, w_gate, w_up, w_down)


def _combine_kernel(final, start_ref, cnt_ref, x_ref, lr_ref, gate_ref, ye_hbm, *rest):
    if final:
        ln_ref, o_ref, slab, sems = rest
    else:
        o_ref, slab, sems = rest
    j = pl.program_id(0)
    nt = pl.num_programs(0)
    buf = lax.rem(j, 2)
    extra = 2
    lr = lr_ref[0]
    gate = gate_ref[0]
    g_hi = gate.astype(BF16).astype(F32)
    g_lo = gate - g_hi
    shifts = [_slot_geometry(start_ref, cnt_ref, j, e)[1] for e in range(N_EXPERTS)]
    rounds = _slot_rounds(start_ref, cnt_ref, j)

    def copy(jj, e, rnd, b):
        abase = _slot_geometry(start_ref, cnt_ref, jj, e)[0]
        src = pl.multiple_of(abase + rnd * MOE_SLOTS, MOE_SLOT_ALIGN)
        return pltpu.make_async_copy(ye_hbm.at[e, pl.ds(src, MOE_SLOTS)], slab.at[b, e], sems.at[b, e])

    def each(jj, rnd, fn, fn_idle=None):
        for e in range(N_EXPERTS):
            _, _, cnt, total = _slot_geometry(start_ref, cnt_ref, jj, e)
            active = (cnt > 0) & (total > rnd * MOE_SLOTS)

            @pl.when(active)
            def _():
                fn(e)

            if fn_idle is not None:
                @pl.when(jnp.logical_not(active))
                def _():
                    fn_idle(e)

    def window(rnd, b):
        w_hi = _slot_one_hot(lr, shifts, rnd, g_hi).astype(BF16)
        w_lo = _slot_one_hot(lr, shifts, rnd, g_lo).astype(BF16)

        def idle(e):
            slab[b, e] = jnp.zeros((MOE_SLOTS, D_MODEL), BF16)

        each(j, rnd, lambda e: copy(j, e, rnd, b).wait(), idle)
        ye = slab[b].reshape(N_EXPERTS * MOE_SLOTS, D_MODEL)
        tn = lambda a, c: lax.dot_general(a, c, (((0,), (0,)), ((), ())), preferred_element_type=F32)
        return tn(w_hi, ye) + tn(w_lo, ye)

    @pl.when(j == 0)
    def _():
        each(j, 0, lambda e: copy(j, e, 0, buf).start())

    jn = jnp.minimum(j + 1, nt - 1)

    @pl.when(j + 1 < nt)
    def _():
        each(jn, 0, lambda e: copy(jn, e, 0, 1 - buf).start())

    y = window(0, buf)

    def more(r, acc):
        each(j, r, lambda e: copy(j, e, r, extra).start())
        return acc + window(r, extra)

    y = lax.cond(rounds > 1, lambda: lax.fori_loop(1, rounds, more, y), lambda: y)
    x2 = x_ref[...] + y
    o_ref[...] = _rms(x2, ln_ref[...]) if final else x2


def _combine(base, cnt, x1, lr, gate3, ye, ln_f):
    n = x1.shape[0]
    nt = n // MOE_TOKEN_TILE
    final = ln_f is not None
    tok = pl.BlockSpec((MOE_TOKEN_TILE, D_MODEL), lambda j, b, c: (j, 0))
    sel = pl.BlockSpec((1, N_EXPERTS, MOE_TOKEN_TILE), lambda j, b, c: (j, 0, 0))
    in_specs = [tok, sel, sel, pl.BlockSpec(memory_space=pl.ANY)]
    args = [base, cnt, x1, lr, gate3, ye]
    if final:
        in_specs.append(pl.BlockSpec((1, D_MODEL), lambda j, b, c: (0, 0)))
        args.append(ln_f)
    return pl.pallas_call(
        functools.partial(_combine_kernel, final),
        grid_spec=pltpu.PrefetchScalarGridSpec(
            num_scalar_prefetch=2, grid=(nt,), in_specs=in_specs, out_specs=tok,
            scratch_shapes=[pltpu.VMEM((3, N_EXPERTS, MOE_SLOTS, D_MODEL), BF16),
                            pltpu.SemaphoreType.DMA((3, N_EXPERTS))]),
        out_shape=jax.ShapeDtypeStruct((n, D_MODEL), F32),
        compiler_params=_cparams(("arbitrary",)),
        name="moe_combine_final" if final else "moe_combine")(*args)


def _moe(x1, h_bf, aff3, layer, w_gate, w_up, w_down, ln_f):
    n = h_bf.shape[0]
    cap = EC_CAPACITY * n // N_EXPERTS
    lr, gate3, tbl = _select(aff3, cap)
    start, cnt = tbl[:, :, 0], tbl[:, :, 1]
    xe = _dispatch(start, cnt, h_bf, lr, cap)
    ye = _expert_ffn(xe, cap, layer, w_gate, w_up, w_down)
    return _combine(start, cnt, x1, lr, gate3, ye, ln_f)


def _prep_layer(l, ln1, w_in, conv_w, a_log, dt_bias, gdn_norm, rpb, w_out, ln2, w_router):
    w_pad = jnp.pad(w_in[l], ((0, 0), (0, IN_WIDTH_PAD - IN_WIDTH))).astype(BF16)
    return dict(
        ln1=ln1[l].reshape(1, D_MODEL), w_pad=w_pad,
        wgt=jnp.transpose(w_in[l][:, GATE_COL0:IN_WIDTH]).astype(BF16),
        conv_w=jnp.pad(conv_w[l], ((0, 8 - CONV_K), (0, 0))),
        a_log=a_log[l], dt_bias=dt_bias[l], gdn_norm=gdn_norm[l].reshape(1, GDN_HEAD_DIM),
        bias_tab=_na_bias_table(rpb[l]), w_out=w_out[l].astype(BF16), ln2=ln2[l].reshape(1, D_MODEL),
        w_router_t=jnp.transpose(w_router[l]))


def _trunk(x3, layers, experts, ln_f):
    batch, seq, _ = x3.shape
    x = x3.reshape(batch * seq, D_MODEL)
    for li, p in enumerate(layers):
        q, k, v, c, z, gate, gate_t = _inproj(x, p["ln1"], p["w_pad"], p["wgt"])
        na = _neighbourhood_attention(q, k, v, p["bias_tab"], batch)
        gq, gk, gv = _gdn_prep(c, p["conv_w"], batch)
        o_f, o_b = _gdn_scan(gq, gk, gv, gate, gate_t, p["a_log"], p["dt_bias"], batch)
        x1, h_bf, aff3 = _outproj(x, na, o_f, o_b, z, p["gdn_norm"], p["w_out"], p["ln2"], p["w_router_t"])
        last = li == len(layers) - 1
        x = _moe(x1, h_bf, aff3, li, *experts, ln_f.reshape(1, D_MODEL) if last else None)
    return x.reshape(batch, seq, D_MODEL)


def kernel(x_prompt, x_sample, ln1, w_in, conv_w, a_log, dt_bias, gdn_norm, rpb, w_out, ln2, w_router, w_gate, w_up, w_down, ln_f):
    layers = [_prep_layer(l, ln1, w_in, conv_w, a_log, dt_bias, gdn_norm, rpb, w_out, ln2, w_router)
              for l in range(ln1.shape[0])]
    experts = (w_gate, w_up, w_down)
    return (_trunk(x_prompt, layers, experts, ln_f), _trunk(x_sample, layers, experts, ln_f))
```

```python
import functools

import jax
import jax.numpy as jnp
from jax import lax
from jax.experimental import pallas as pl
from jax.experimental.pallas import tpu as pltpu

F32 = jnp.float32
BF16 = jnp.bfloat16

D_MODEL = 1024
GRID_W = 64
NA_HEADS = 8
NA_HEAD_DIM = 64
NA_WIDTH = 512
NA_ROWS = 8
NA_COLS = 16
GDN_HEADS = 4
GDN_HEAD_DIM = 128
GDN_WIDTH = 512
CONV_K = 5
CHUNK = 64
IN_WIDTH = 3600
N_EXPERTS = 16
EXPERT_FF = 1024
EC_CAPACITY = 2
EPS = 1e-6

LANES = 128
IN_WIDTH_PAD = 3712
GATE_COL0 = 3584
NEG_BIG = -1e30
VMEM_LIMIT = 56 * 1024 * 1024

ROW_TILE = 512
NA_ROW_BLOCK = 32
GDN_BLOCK_CHUNKS = 8
FFN_ROW_TILE = 512
MOE_TOKEN_TILE = 256
MOE_SLOTS = 64
MOE_SLOT_SHIFT = 4
MOE_SLOT_ALIGN = 1 << MOE_SLOT_SHIFT


def _cparams(sem):
    return pltpu.CompilerParams(dimension_semantics=sem, vmem_limit_bytes=VMEM_LIMIT)


def _bdot(a, b):
    return jnp.dot(a.astype(BF16), b.astype(BF16), preferred_element_type=F32)


def _bdot_nt(a, b):
    return lax.dot_general(a.astype(BF16), b.astype(BF16), (((1,), (1,)), ((), ())),
                           preferred_element_type=F32)


def _bdot_tn(a, b):
    return lax.dot_general(a.astype(BF16), b.astype(BF16), (((0,), (0,)), ((), ())),
                           preferred_element_type=F32)


def _split3(x):
    x1 = x.astype(BF16)
    r1 = x - x1.astype(F32)
    x2 = r1.astype(BF16)
    r2 = r1 - x2.astype(F32)
    return x1, x2, r2.astype(BF16)


def _silu(x):
    return x * (1.0 / (1.0 + jnp.exp(-x)))


def _sigmoid(x):
    return 1.0 / (1.0 + jnp.exp(-x))


def _softplus(x):
    return jnp.maximum(x, 0.0) + jnp.log(1.0 + jnp.exp(-jnp.abs(x)))


def _rms(x, g):
    ms = jnp.mean(x * x, axis=-1, keepdims=True)
    return (x * lax.rsqrt(ms + EPS)) * g


def _block_diag(a, b):
    za, zb = jnp.zeros_like(a), jnp.zeros_like(b)
    return jnp.concatenate([jnp.concatenate([a, zb], axis=1), jnp.concatenate([za, b], axis=1)], axis=0)


def _inproj_kernel(x_ref, ln_ref, w_ref, wgt_ref, q_out, k_out, v_out, c_out, z_out, g_out, gt_out):
    h = _rms(x_ref[...], ln_ref[...]).astype(BF16)
    q_out[...] = jnp.dot(h, w_ref[:, 0:512], preferred_element_type=F32).astype(BF16)
    k_out[...] = jnp.dot(h, w_ref[:, 512:1024], preferred_element_type=F32).astype(BF16)
    v_out[...] = jnp.dot(h, w_ref[:, 1024:1536], preferred_element_type=F32).astype(BF16)
    c_out[...] = jnp.dot(h, w_ref[:, 1536:3072], preferred_element_type=F32)
    z_out[...] = jnp.dot(h, w_ref[:, 3072:3584], preferred_element_type=F32)
    g_out[...] = jnp.dot(h, w_ref[:, GATE_COL0:IN_WIDTH_PAD], preferred_element_type=F32)
    gt_out[...] = lax.dot_general(wgt_ref[...], h, (((1,), (1,)), ((), ())), preferred_element_type=F32)


def _inproj(x, ln, w_pad, wgt):
    n = x.shape[0]
    tm = ROW_TILE
    row = lambda w: pl.BlockSpec((tm, w), lambda i: (i, 0))
    const = lambda s: pl.BlockSpec(s, lambda i: (0, 0))
    out_shape = [
        jax.ShapeDtypeStruct((n, NA_WIDTH), BF16), jax.ShapeDtypeStruct((n, NA_WIDTH), BF16),
        jax.ShapeDtypeStruct((n, NA_WIDTH), BF16), jax.ShapeDtypeStruct((n, 3 * GDN_WIDTH), F32),
        jax.ShapeDtypeStruct((n, GDN_WIDTH), F32), jax.ShapeDtypeStruct((n, LANES), F32),
        jax.ShapeDtypeStruct((16, n), F32)]
    out_specs = [row(NA_WIDTH), row(NA_WIDTH), row(NA_WIDTH), row(3 * GDN_WIDTH), row(GDN_WIDTH), row(LANES),
                 pl.BlockSpec((16, tm), lambda i: (0, i))]
    return pl.pallas_call(
        _inproj_kernel, grid=(n // tm,),
        in_specs=[row(D_MODEL), const((1, D_MODEL)), const((D_MODEL, IN_WIDTH_PAD)), const((16, D_MODEL))],
        out_specs=out_specs, out_shape=out_shape, compiler_params=_cparams(("parallel",)),
        name="inproj")(x, ln, w_pad, wgt)


def _na_kernel(rows, q_ref, k_ref, v_ref, bias_ref, o_ref):
    i = pl.program_id(2)
    lane = lax.broadcasted_iota(jnp.int32, (GRID_W, LANES), 1)
    low = lane < NA_HEAD_DIM
    scale = NA_HEAD_DIM ** -0.5

    def scores(rr):
        r = i * NA_ROW_BLOCK + rr
        rs = jnp.clip(r - NA_ROWS // 2, 0, rows - NA_ROWS)
        d0 = rs - r + NA_ROWS - 1
        k0 = pl.multiple_of(rs * GRID_W, GRID_W)
        kw = k_ref[0, pl.ds(k0, NA_ROWS * GRID_W), :]
        q = q_ref[0, rr * GRID_W:(rr + 1) * GRID_W, :] * scale
        zq = jnp.zeros_like(q)
        q2 = jnp.concatenate([jnp.where(low, q, zq), jnp.where(low, zq, q)], axis=0)
        bias = jnp.concatenate([bias_ref[0, d0], bias_ref[1, d0]], axis=0)
        return _bdot_nt(q2, kw) + bias, k0

    def softmax(s, k0):
        m = jnp.max(s, axis=-1, keepdims=True)
        p = jnp.exp(s - m)
        return p.astype(BF16), 1.0 / jnp.sum(p, axis=-1, keepdims=True), k0

    def finish(rr, p, inv, k0):
        vw = v_ref[0, pl.ds(k0, NA_ROWS * GRID_W), :]
        o2 = jnp.dot(p, vw, preferred_element_type=F32) * inv
        o_ref[0, rr * GRID_W:(rr + 1) * GRID_W, :] = jnp.where(low, o2[:GRID_W], o2[GRID_W:]).astype(o_ref.dtype)

    s_q = [scores(0), scores(1)]
    p_q = [softmax(*s_q.pop(0))]
    for rr in range(NA_ROW_BLOCK):
        if rr + 2 < NA_ROW_BLOCK:
            s_q.append(scores(rr + 2))
        if s_q:
            p_q.append(softmax(*s_q.pop(0)))
        finish(rr, *p_q.pop(0))


def _na_bias_table(rpb):
    col = jnp.arange(GRID_W)
    cs = jnp.clip(col - NA_COLS // 2, 0, GRID_W - NA_COLS)
    kcol = jnp.arange(GRID_W)
    valid = (kcol[None, :] >= cs[:, None]) & (kcol[None, :] < cs[:, None] + NA_COLS)
    cidx = jnp.clip(kcol[None, :] - col[:, None] + NA_COLS - 1, 0, 2 * NA_COLS - 2)
    band = jnp.where(valid[None, None], rpb[:, :, cidx], NEG_BIG)
    ridx = jnp.arange(NA_ROWS)[:, None] + jnp.arange(NA_ROWS)[None, :]
    tab = band[:, ridx]
    tab = jnp.transpose(tab, (0, 1, 3, 2, 4))
    return tab.reshape(NA_HEADS, NA_ROWS, GRID_W, NA_ROWS * GRID_W).astype(F32)


def _neighbourhood_attention(q, k, v, bias_tab, batch):
    n = q.shape[0]
    seq = n // batch
    rows = seq // GRID_W
    q3, k3, v3 = (t.reshape(batch, seq, NA_WIDTH) for t in (q, k, v))
    assert rows % NA_ROW_BLOCK == 0, (rows, NA_ROW_BLOCK)
    tq = NA_ROW_BLOCK * GRID_W
    out = pl.pallas_call(
        functools.partial(_na_kernel, rows),
        grid=(batch, NA_HEADS // 2, rows // NA_ROW_BLOCK),
        in_specs=[pl.BlockSpec((1, tq, LANES), lambda b, hp, i: (b, i, hp)),
                  pl.BlockSpec((1, seq, LANES), lambda b, hp, i: (b, 0, hp)),
                  pl.BlockSpec((1, seq, LANES), lambda b, hp, i: (b, 0, hp)),
                  pl.BlockSpec((2, NA_ROWS, GRID_W, NA_ROWS * GRID_W), lambda b, hp, i: (hp, 0, 0, 0))],
        out_specs=pl.BlockSpec((1, tq, LANES), lambda b, hp, i: (b, i, hp)),
        out_shape=jax.ShapeDtypeStruct((batch, seq, NA_WIDTH), BF16),
        compiler_params=_cparams(("parallel", "parallel", "arbitrary")),
        name="na_attention")(q3, k3, v3, bias_tab)
    return out.reshape(n, NA_WIDTH)


def _gdn_prep_kernel(nblk, c_ref, prev_ref, next_ref, w_ref, q_out, k_out, v_out, xs):
    i = pl.program_id(1)
    tm = c_ref.shape[1]
    pad = CONV_K // 2
    halo = 8
    xs[0:halo, :] = jnp.where(i > 0, prev_ref[0], 0.0)
    xs[halo:halo + tm, :] = c_ref[0]
    xs[halo + tm:, :] = jnp.where(i < nblk - 1, next_ref[0], 0.0)
    outs = (q_out, k_out, v_out)
    for blk in range(3 * GDN_HEADS):
        sl = slice(blk * GDN_HEAD_DIM, (blk + 1) * GDN_HEAD_DIM)
        acc = xs[halo - pad:halo - pad + tm, sl] * w_ref[0:1, sl]
        for j in range(1, CONV_K):
            acc = acc + xs[halo - pad + j:halo - pad + j + tm, sl] * w_ref[j:j + 1, sl]
        c = _silu(acc)
        which, hd = divmod(blk, GDN_HEADS)
        if which < 2:
            c = c * lax.rsqrt(jnp.sum(c * c, axis=-1, keepdims=True) + EPS)
        outs[which][0, :, hd * GDN_HEAD_DIM:(hd + 1) * GDN_HEAD_DIM] = c.astype(outs[which].dtype)


def _gdn_prep(c, conv_w, batch):
    n = c.shape[0]
    seq = n // batch
    tm = ROW_TILE
    nblk = seq // tm
    c3 = c.reshape(batch, seq, 3 * GDN_WIDTH)
    hb = tm // 8
    spec = pl.BlockSpec((1, tm, GDN_WIDTH), lambda b, i: (b, i, 0))
    outs = pl.pallas_call(
        functools.partial(_gdn_prep_kernel, nblk),
        grid=(batch, nblk),
        in_specs=[pl.BlockSpec((1, tm, 3 * GDN_WIDTH), lambda b, i: (b, i, 0)),
                  pl.BlockSpec((1, 8, 3 * GDN_WIDTH), lambda b, i: (b, jnp.maximum(i * hb - 1, 0), 0)),
                  pl.BlockSpec((1, 8, 3 * GDN_WIDTH), lambda b, i: (b, jnp.minimum((i + 1) * hb, seq // 8 - 1), 0)),
                  pl.BlockSpec((8, 3 * GDN_WIDTH), lambda b, i: (0, 0))],
        out_specs=[spec, spec, spec],
        out_shape=[jax.ShapeDtypeStruct((batch, seq, GDN_WIDTH), BF16)] * 3,
        scratch_shapes=[pltpu.VMEM((tm + 16, 3 * GDN_WIDTH), F32)],
        compiler_params=_cparams(("parallel", "parallel")),
        name="gdn_prep")(c3, c3, c3, conv_w)
    return tuple(t.reshape(n, GDN_WIDTH) for t in outs)


def _gdn_scan_kernel(qf, kf, vf, gf, gtf, qb, kb, vb, gb, gtb, arow, drow, acol, dcol, of_ref, ob_ref, s_ref):
    @pl.when(pl.program_id(1) == 0)
    def _():
        s_ref[...] = jnp.zeros_like(s_ref)

    cb = GDN_BLOCK_CHUNKS
    hw = GDN_HEAD_DIM
    npair = GDN_HEADS // 2
    ri = lax.broadcasted_iota(jnp.int32, (CHUNK, LANES), 0)
    lane = lax.broadcasted_iota(jnp.int32, (CHUNK, LANES), 1)
    lo = lane < CHUNK
    cm = jnp.bitwise_and(lane, CHUNK - 1)
    eye = jnp.where(ri == cm, 1.0, 0.0)
    rr = lax.broadcasted_iota(jnp.int32, (CHUNK, CHUNK), 0)
    cc = lax.broadcasted_iota(jnp.int32, (CHUNK, CHUNK), 1)
    neg_a_row, neg_a_col = -jnp.exp(arow[...]), -jnp.exp(acol[...])
    d_row, d_col = drow[...], dcol[...]
    refs = ((qf, kf, vf, gf, gtf, of_ref), (qb, kb, vb, gb, gtb, ob_ref))
    incl = (ri >= cm, ri <= cm)
    strict = (ri > cm, ri < cm)
    tri = (jnp.where(rr >= cc, 1.0, 0.0).astype(BF16), jnp.where(rr <= cc, 1.0, 0.0).astype(BF16))
    tri_t2 = (jnp.where(cm >= ri, 1.0, 0.0).astype(BF16), jnp.where(cm <= ri, 1.0, 0.0).astype(BF16))
    f32dot = lambda a, b: jnp.dot(a, b, preferred_element_type=F32)

    inst = [(d, c) for d in (0, 1) for c in range(cb)]
    pairs = [(d, c, p) for d, c in inst for p in range(npair)]

    st = {}
    for d, c in inst:
        q_ref, k_ref, v_ref, g_ref, gt_ref, _ = refs[d]
        sl = slice(c * CHUNK, (c + 1) * CHUNK)
        graw = g_ref[0, sl, :]
        g1, g2, g3 = _split3(neg_a_row * _softplus(graw + d_row))
        t1, t2, t3 = _split3(neg_a_col * _softplus(gt_ref[0, c] + d_col))
        st[d, c] = dict(
            gc_full=f32dot(tri[d], g1) + f32dot(tri[d], g2) + f32dot(tri[d], g3),
            gct2=f32dot(t1, tri_t2[d]) + f32dot(t2, tri_t2[d]) + f32dot(t3, tri_t2[d]),
            beta_full=_sigmoid(graw),
            q=q_ref[0, sl, :].astype(F32), k=k_ref[0, sl, :].astype(F32), v=v_ref[0, sl, :].astype(F32))

    pk = {}
    for d, c, p in pairs:
        s = st[d, c]
        h0, h1 = 2 * p, 2 * p + 1
        col = lambda full, base: jnp.where(lo, full[:, base + h0:base + h0 + 1], full[:, base + h1:base + h1 + 1])
        gc = col(s["gc_full"], 8 + 4 * d)
        beta = col(s["beta_full"], 4 * d)
        g_row0 = 8 + 4 * d + h0
        gr = jnp.where(lo[0:1], s["gct2"][g_row0:g_row0 + 1, :], s["gct2"][g_row0 + 1:g_row0 + 2, :])
        hs = lambda t, h: t[:, h * hw:(h + 1) * hw]
        q0, q1 = hs(s["q"], h0) * (hw ** -0.5), hs(s["q"], h1) * (hw ** -0.5)
        k0, k1 = hs(s["k"], h0), hs(s["k"], h1)
        lhs = jnp.concatenate([jnp.concatenate([q0, k0], axis=0), jnp.concatenate([q1, k1], axis=0)], axis=1)
        qkk = _bdot_nt(lhs, _block_diag(k0.astype(BF16), k1.astype(BF16)))
        decay = jnp.where(incl[d], jnp.exp(jnp.where(incl[d], gc - gr, 0.0)), 0.0)
        last = CHUNK - 1 if d == 0 else 0
        g_last = gc[last:last + 1, :]
        pk[d, c, p] = dict(qk=qkk[:CHUNK] * decay, pw=jnp.where(strict[d], qkk[CHUNK:] * beta * decay, 0.0),
                           beta=beta, egc=jnp.exp(gc), g_last=g_last, dk=jnp.exp(g_last - gc),
                           q=(q0, q1), k=(k0, k1), v=(hs(s["v"], h0), hs(s["v"], h1)))

    def bd_of(pw):
        z = jnp.zeros_like(pw)
        return jnp.concatenate([jnp.where(lo, pw, z), jnp.where(lo, z, pw)], axis=0)

    for key in pairs:
        e = pk[key]
        e["tinv"] = eye - e["pw"]
        pw = e["pw"].astype(BF16)
        e["pw"] = f32dot(pw, bd_of(pw))
    for level in range(1, 6):
        for key in pairs:
            e = pk[key]
            pw = e["pw"].astype(BF16)
            if level < 5:
                r = f32dot(jnp.concatenate([pw, e["tinv"].astype(BF16)], axis=0), bd_of(pw))
                e["pw"] = r[:CHUNK]
                e["tinv"] = e["tinv"] + r[CHUNK:]
            else:
                e["tinv"] = e["tinv"] + f32dot(e["tinv"].astype(BF16), bd_of(pw))

    for key in pairs:
        e = pk[key]
        halves = lambda t: (t[:, 0:1], t[:, LANES - 1:LANES])
        b, g, dk = halves(e["beta"]), halves(e["egc"]), halves(e["dk"])
        rhs = [jnp.concatenate([e["v"][hh] * b[hh], e["k"][hh] * (b[hh] * g[hh])], axis=1).astype(BF16)
               for hh in range(2)]
        uw = f32dot(e["tinv"].astype(BF16), _block_diag(rhs[0], rhs[1]))
        e["u"] = (uw[:, 0:hw], uw[:, 2 * hw:3 * hw])
        e["w"] = (uw[:, hw:2 * hw], uw[:, 3 * hw:4 * hw])
        e["qg"] = (e["q"][0] * g[0], e["q"][1] * g[1])
        e["kg"] = (e["k"][0] * dk[0], e["k"][1] * dk[1])
        e["sdec"] = (jnp.exp(e["g_last"][:, 0:1]), jnp.exp(e["g_last"][:, LANES - 1:LANES]))

    for j in range(cb):
        step = [(0, j, p) for p in range(npair)] + [(1, cb - 1 - j, p) for p in range(npair)]
        s_old, wsqs, v_new = {}, {}, {}
        for d, c, p in step:
            e = pk[d, c, p]
            for hh in range(2):
                s_old[d, p, hh] = s_ref[d, 2 * p + hh]
                wsqs[d, p, hh] = _bdot(jnp.concatenate([e["w"][hh], e["qg"][hh]], axis=0), s_old[d, p, hh])
        for d, c, p in step:
            for hh in range(2):
                v_new[d, p, hh] = pk[d, c, p]["u"][hh] - wsqs[d, p, hh][:CHUNK]
        for d, c, p in step:
            e = pk[d, c, p]
            bdv = _block_diag(v_new[d, p, 0].astype(BF16), v_new[d, p, 1].astype(BF16))
            o_pair = f32dot(e["qk"].astype(BF16), bdv)
            o_pair = o_pair + jnp.concatenate([wsqs[d, p, 0][CHUNK:], wsqs[d, p, 1][CHUNK:]], axis=1)
            refs[d][5][0, c * CHUNK:(c + 1) * CHUNK, 2 * p * hw:(2 * p + 2) * hw] = o_pair
            for hh in range(2):
                s_ref[d, 2 * p + hh] = s_old[d, p, hh] * e["sdec"][hh] + _bdot_tn(e["kg"][hh], v_new[d, p, hh])


def _gdn_scan(q, k, v, gate, gate_t, a_log, dt_bias, batch):
    n = q.shape[0]
    seq = n // batch
    tb = GDN_BLOCK_CHUNKS * CHUNK
    nblk = seq // tb
    q3, k3, v3 = (t.reshape(batch, seq, GDN_WIDTH) for t in (q, k, v))
    g3 = gate.reshape(batch, seq, LANES)
    gt4 = jnp.transpose(gate_t.reshape(16, batch, seq // CHUNK, CHUNK), (1, 2, 0, 3))
    arow = jnp.zeros((1, LANES), F32).at[0, 8:16].set(a_log.reshape(-1))
    drow = jnp.zeros((1, LANES), F32).at[0, 8:16].set(dt_bias.reshape(-1))
    acol = arow[0, :16].reshape(16, 1)
    dcol = drow[0, :16].reshape(16, 1)
    fwd = lambda w: pl.BlockSpec((1, tb, w), lambda b, i: (b, i, 0))
    bwd = lambda w: pl.BlockSpec((1, tb, w), lambda b, i: (b, nblk - 1 - i, 0))
    gtf = pl.BlockSpec((1, GDN_BLOCK_CHUNKS, 16, CHUNK), lambda b, i: (b, i, 0, 0))
    gtb = pl.BlockSpec((1, GDN_BLOCK_CHUNKS, 16, CHUNK), lambda b, i: (b, nblk - 1 - i, 0, 0))
    const = lambda s: pl.BlockSpec(s, lambda b, i: (0, 0))
    o_f, o_b = pl.pallas_call(
        _gdn_scan_kernel, grid=(batch, nblk),
        in_specs=[fwd(GDN_WIDTH), fwd(GDN_WIDTH), fwd(GDN_WIDTH), fwd(LANES), gtf,
                  bwd(GDN_WIDTH), bwd(GDN_WIDTH), bwd(GDN_WIDTH), bwd(LANES), gtb,
                  const((1, LANES)), const((1, LANES)), const((16, 1)), const((16, 1))],
        out_specs=[fwd(GDN_WIDTH), bwd(GDN_WIDTH)],
        out_shape=[jax.ShapeDtypeStruct((batch, seq, GDN_WIDTH), F32)] * 2,
        scratch_shapes=[pltpu.VMEM((2, GDN_HEADS, GDN_HEAD_DIM, GDN_HEAD_DIM), F32)],
        compiler_params=_cparams(("parallel", "arbitrary")),
        name="gdn_scan")(q3, k3, v3, g3, gt4, q3, k3, v3, g3, gt4, arow, drow, acol, dcol)
    return o_f.reshape(n, GDN_WIDTH), o_b.reshape(n, GDN_WIDTH)


def _outproj_kernel(x_ref, na_ref, of_ref, ob_ref, z_ref, gn_ref, w_ref, ln_ref, wr_ref, x1_out, h_out, aff_out):
    o = of_ref[...] + ob_ref[...]
    z = z_ref[...]
    gn = gn_ref[...]
    acc = jnp.dot(na_ref[...], w_ref[0:NA_WIDTH, :], preferred_element_type=F32)
    for hd in range(GDN_HEADS):
        sl = slice(hd * GDN_HEAD_DIM, (hd + 1) * GDN_HEAD_DIM)
        oh = o[:, sl]
        oh = oh * lax.rsqrt(jnp.mean(oh * oh, axis=-1, keepdims=True) + EPS) * gn
        oh = oh * _silu(z[:, sl])
        acc = acc + jnp.dot(oh.astype(BF16), w_ref[NA_WIDTH + hd * GDN_HEAD_DIM:NA_WIDTH + (hd + 1) * GDN_HEAD_DIM, :],
                            preferred_element_type=F32)
    x1 = x_ref[...] + acc
    x1_out[...] = x1
    h = _rms(x1, ln_ref[...])
    h_out[...] = h.astype(BF16)
    h1, h2, _ = _split3(h)
    w1, w2, _ = _split3(wr_ref[...])
    nt = lambda a, b: lax.dot_general(a, b, (((1,), (1,)), ((), ())), preferred_element_type=F32)
    logits = nt(w1, h1) + (nt(w1, h2) + nt(w2, h1))
    m = jnp.max(logits, axis=0, keepdims=True)
    e = jnp.exp(logits - m)
    aff = e / jnp.sum(e, axis=0, keepdims=True)
    for t in range(ROW_TILE // MOE_TOKEN_TILE):
        aff_out[t] = aff[:, t * MOE_TOKEN_TILE:(t + 1) * MOE_TOKEN_TILE]


def _outproj(x, na, o_f, o_b, z, gdn_norm, w_out, ln2, w_router_t):
    n = x.shape[0]
    tm = ROW_TILE
    sub = tm // MOE_TOKEN_TILE
    row = lambda w: pl.BlockSpec((tm, w), lambda i: (i, 0))
    const = lambda s: pl.BlockSpec(s, lambda i: (0, 0))
    return pl.pallas_call(
        _outproj_kernel, grid=(n // tm,),
        in_specs=[row(D_MODEL), row(NA_WIDTH), row(GDN_WIDTH), row(GDN_WIDTH), row(GDN_WIDTH),
                  const((1, GDN_HEAD_DIM)), const((D_MODEL, D_MODEL)), const((1, D_MODEL)), const((N_EXPERTS, D_MODEL))],
        out_specs=[row(D_MODEL), row(D_MODEL),
                   pl.BlockSpec((sub, N_EXPERTS, MOE_TOKEN_TILE), lambda i: (i, 0, 0))],
        out_shape=[jax.ShapeDtypeStruct((n, D_MODEL), F32), jax.ShapeDtypeStruct((n, D_MODEL), BF16),
                   jax.ShapeDtypeStruct((n // MOE_TOKEN_TILE, N_EXPERTS, MOE_TOKEN_TILE), F32)],
        compiler_params=_cparams(("parallel",)),
        name="outproj_router")(x, na, o_f, o_b, z, gdn_norm, w_out, ln2, w_router_t)


def _select_kernel(cap, aff_ref, lr_out, gate_out, tbl_out):
    aff = aff_ref[...]
    nt = aff.shape[0]
    count = lambda m: jnp.sum(jnp.sum(jnp.where(m, 1.0, 0.0), axis=0), axis=-1, keepdims=True)
    capf = jnp.float32(cap)
    as_f32 = lambda b: pltpu.bitcast(b, F32)

    def bit_step(i, thr):
        cand = jnp.bitwise_or(thr, jnp.left_shift(jnp.int32(1), 30 - i))
        return jnp.where(count(aff >= as_f32(cand)[None]) >= capf, cand, thr)

    thr = lax.fori_loop(0, 31, bit_step, jnp.zeros((N_EXPERTS, 1), jnp.int32))
    lo, hi = as_f32(thr), as_f32(thr + 1)

    def mid_step(_, lh):
        lo, hi = lh
        mid = 0.5 * (lo + hi)
        up = count(aff >= mid[None]) >= capf
        return jnp.where(up, mid, lo), jnp.where(up, hi, mid)

    lo, hi = lax.fori_loop(0, 32, mid_step, (lo, hi))
    gt = aff >= hi[None]
    eq = (aff >= lo[None]) & jnp.logical_not(gt)
    need = capf - count(gt)
    tok = (lax.broadcasted_iota(jnp.int32, aff.shape, 0) * MOE_TOKEN_TILE
           + lax.broadcasted_iota(jnp.int32, aff.shape, 2))

    def tie_step(i, bound):
        cand = bound + jnp.left_shift(jnp.int32(1), 15 - i)
        return jnp.where(count(eq & (tok < cand[None])) <= need, cand, bound)

    bound = lax.fori_loop(0, 16, tie_step, jnp.zeros((N_EXPERTS, 1), jnp.int32))
    mask = gt | (eq & (tok < bound[None]))
    maskf = jnp.where(mask, 1.0, 0.0)
    t_r = lax.broadcasted_iota(jnp.int32, (MOE_TOKEN_TILE, MOE_TOKEN_TILE), 0)
    t_c = lax.broadcasted_iota(jnp.int32, (MOE_TOKEN_TILE, MOE_TOKEN_TILE), 1)
    before = jnp.where(t_r < t_c, 1.0, 0.0).astype(BF16)
    rank = jnp.dot(maskf.astype(BF16).reshape(nt * N_EXPERTS, MOE_TOKEN_TILE), before,
                   preferred_element_type=F32).reshape(aff.shape)
    lr_out[...] = jnp.where(mask, rank + 1.0, 0.0)
    gate_out[...] = jnp.where(mask, aff, 0.0)
    cnt = jnp.sum(maskf, axis=-1, keepdims=True).astype(jnp.int32)
    run = cnt
    s = 1
    while s < nt:
        run = run + jnp.concatenate([jnp.zeros((s,) + run.shape[1:], jnp.int32), run[:nt - s]], axis=0)
        s *= 2
    lane = lax.broadcasted_iota(jnp.int32, tbl_out.shape, 2)
    tbl_out[...] = jnp.where(lane == 0, run - cnt, jnp.where(lane == 1, cnt, 0))


def _select(aff3, cap):
    nt = aff3.shape[0]
    full = lambda w: pl.BlockSpec((nt, N_EXPERTS, w), lambda: (0, 0, 0))
    return pl.pallas_call(
        functools.partial(_select_kernel, cap), in_specs=[full(MOE_TOKEN_TILE)],
        out_specs=[full(MOE_TOKEN_TILE), full(MOE_TOKEN_TILE), full(LANES)],
        out_shape=[jax.ShapeDtypeStruct(aff3.shape, F32), jax.ShapeDtypeStruct(aff3.shape, F32),
                   jax.ShapeDtypeStruct((nt, N_EXPERTS, LANES), jnp.int32)],
        compiler_params=pltpu.CompilerParams(vmem_limit_bytes=VMEM_LIMIT),
        name="moe_select")(aff3)


def _slot_geometry(start_ref, cnt_ref, j, e):
    start, cnt = start_ref[j, e], cnt_ref[j, e]
    shift = jnp.bitwise_and(start, MOE_SLOT_ALIGN - 1)
    return start - shift, shift, cnt, shift + cnt


def _slot_rounds(start_ref, cnt_ref, j):
    rounds = jnp.int32(0)
    for e in range(N_EXPERTS):
        _, _, cnt, total = _slot_geometry(start_ref, cnt_ref, j, e)
        rounds = jnp.maximum(rounds, jnp.where(cnt > 0, (total + MOE_SLOTS - 1) // MOE_SLOTS, 0))
    return rounds


def _slot_one_hot(lr, shifts, rnd, val):
    p1 = (lax.broadcasted_iota(jnp.int32, (MOE_SLOTS, MOE_TOKEN_TILE), 0) + 1 + rnd * MOE_SLOTS).astype(F32)
    rows = []
    for e in range(N_EXPERTS):
        lre = lr[e:e + 1, :]
        pos = jnp.where(lre > 0.0, lre + shifts[e].astype(F32), 0.0)
        v = 1.0 if val is None else val[e:e + 1, :]
        rows.append(jnp.where(pos == p1, v, 0.0))
    return jnp.concatenate(rows, axis=0)


def _dispatch_kernel(cap, start_ref, cnt_ref, h_ref, lr_ref, xe_hbm, stage, tail, sems):
    j = pl.program_id(0)
    nt = pl.num_programs(0)
    buf = lax.rem(j, 2)
    lr = lr_ref[0]
    h = h_ref[...]
    geo = [_slot_geometry(start_ref, cnt_ref, j, e) for e in range(N_EXPERTS)]
    rounds = _slot_rounds(start_ref, cnt_ref, j)

    def copy(jj, e, rnd, b):
        abase = _slot_geometry(start_ref, cnt_ref, jj, e)[0]
        dst = pl.multiple_of(abase + rnd * MOE_SLOTS, MOE_SLOT_ALIGN)
        return pltpu.make_async_copy(stage.at[b, e, pl.ds(0, MOE_SLOTS)], xe_hbm.at[e, pl.ds(dst, MOE_SLOTS)],
                                     sems.at[b, e])

    @pl.when(j == 0)
    def _():
        tail[...] = jnp.zeros_like(tail)
        stage[...] = jnp.zeros_like(stage)
        for e in range(N_EXPERTS):
            pltpu.make_async_copy(stage.at[1, e, pl.ds(0, MOE_SLOTS)], xe_hbm.at[e, pl.ds(cap, MOE_SLOTS)],
                                  sems.at[1, e]).start()
        for e in range(N_EXPERTS):
            pltpu.make_async_copy(stage.at[1, e, pl.ds(0, MOE_SLOTS)], xe_hbm.at[e, pl.ds(cap, MOE_SLOTS)],
                                  sems.at[1, e]).wait()

    def fill(rnd, first):
        oh = _slot_one_hot(lr, [g[1] for g in geo], rnd, None).astype(BF16)
        rows = jnp.dot(oh, h, preferred_element_type=F32)
        for e in range(N_EXPERTS):
            r = rows[e * MOE_SLOTS:(e + 1) * MOE_SLOTS]
            if first:
                r = jnp.concatenate([r[:MOE_SLOT_ALIGN] + tail[e].astype(F32), r[MOE_SLOT_ALIGN:]], axis=0)
            stage[buf, e, 0:MOE_SLOTS, :] = r.astype(BF16)
        for e in range(N_EXPERTS):
            _, _, cnt, total = geo[e]

            @pl.when((cnt > 0) & ((total - 1) // MOE_SLOTS == rnd))
            def _():
                grp = jnp.right_shift(total, MOE_SLOT_SHIFT) - rnd * (MOE_SLOTS // MOE_SLOT_ALIGN)
                tail[e] = stage[buf, e, pl.ds(pl.multiple_of(grp * MOE_SLOT_ALIGN, MOE_SLOT_ALIGN), MOE_SLOT_ALIGN), :]

    def each_active(jj, rnd, fn):
        for e in range(N_EXPERTS):
            _, _, cnt, total = _slot_geometry(start_ref, cnt_ref, jj, e)

            @pl.when((cnt > 0) & (total > rnd * MOE_SLOTS))
            def _():
                fn(e)

    fill(0, True)
    jp = jnp.maximum(j - 1, 0)

    @pl.when((j > 0) & (_slot_rounds(start_ref, cnt_ref, jp) <= 1))
    def _():
        each_active(jp, 0, lambda e: copy(jp, e, 0, 1 - buf).wait())

    each_active(j, 0, lambda e: copy(j, e, 0, buf).start())

    @pl.when(rounds > 1)
    def _():
        each_active(j, 0, lambda e: copy(j, e, 0, buf).wait())

        def more(r, c):
            fill(r, False)
            each_active(j, r, lambda e: copy(j, e, r, buf).start())
            each_active(j, r, lambda e: copy(j, e, r, buf).wait())
            return c

        lax.fori_loop(1, rounds, more, 0)

    @pl.when((j == nt - 1) & (rounds <= 1))
    def _():
        each_active(j, 0, lambda e: copy(j, e, 0, buf).wait())


def _dispatch(start, cnt, h_bf, lr, cap):
    n = h_bf.shape[0]
    nt = n // MOE_TOKEN_TILE
    return pl.pallas_call(
        functools.partial(_dispatch_kernel, cap),
        grid_spec=pltpu.PrefetchScalarGridSpec(
            num_scalar_prefetch=2, grid=(nt,),
            in_specs=[pl.BlockSpec((MOE_TOKEN_TILE, D_MODEL), lambda j, b, c: (j, 0)),
                      pl.BlockSpec((1, N_EXPERTS, MOE_TOKEN_TILE), lambda j, b, c: (j, 0, 0))],
            out_specs=pl.BlockSpec(memory_space=pl.ANY),
            scratch_shapes=[pltpu.VMEM((2, N_EXPERTS, MOE_SLOTS + MOE_SLOT_ALIGN, D_MODEL), BF16),
                            pltpu.VMEM((N_EXPERTS, MOE_SLOT_ALIGN, D_MODEL), BF16),
                            pltpu.SemaphoreType.DMA((2, N_EXPERTS))]),
        out_shape=jax.ShapeDtypeStruct((N_EXPERTS, cap + MOE_SLOTS, D_MODEL), BF16),
        compiler_params=_cparams(("arbitrary",)),
        name="moe_dispatch")(start, cnt, h_bf, lr)


def _ffn_kernel(x_ref, wg_ref, wu_ref, wd_ref, y_out, wg_s, wu_s, wd_s):
    @pl.when(pl.program_id(1) == 0)
    def _():
        wg_s[...] = wg_ref[0, 0].astype(BF16)
        wu_s[...] = wu_ref[0, 0].astype(BF16)
        wd_s[...] = wd_ref[0, 0].astype(BF16)

        y_out[...] = jnp.zeros_like(y_out)

    @pl.when(pl.program_id(1) > 0)
    def _():
        x = x_ref[0]
        a = jnp.dot(x, wg_s[...], preferred_element_type=F32)
        b = jnp.dot(x, wu_s[...], preferred_element_type=F32)
        hid = (_silu(a) * b).astype(BF16)
        y_out[0] = jnp.dot(hid, wd_s[...], preferred_element_type=F32).astype(y_out.dtype)


def _expert_ffn(xe, cap, layer, w_gate, w_up, w_down):
    tm = FFN_ROW_TILE
    ntiles = cap // tm
    wspec = pl.BlockSpec((1, 1, D_MODEL, EXPERT_FF), lambda ei, j: (layer, ei, 0, 0))
    return pl.pallas_call(
        _ffn_kernel, grid=(N_EXPERTS, ntiles + 1),
        in_specs=[pl.BlockSpec((1, tm, D_MODEL), lambda ei, j: (ei, jnp.maximum(j - 1, 0), 0)), wspec, wspec,
                  pl.BlockSpec((1, 1, EXPERT_FF, D_MODEL), lambda ei, j: (layer, ei, 0, 0))],
        out_specs=pl.BlockSpec((1, tm, D_MODEL), lambda ei, j: (ei, jnp.where(j == 0, ntiles, j - 1), 0)),
        out_shape=jax.ShapeDtypeStruct((N_EXPERTS, cap + tm, D_MODEL), BF16),
        scratch_shapes=[pltpu.VMEM((D_MODEL, EXPERT_FF), BF16), pltpu.VMEM((D_MODEL, EXPERT_FF), BF16),
                        pltpu.VMEM((EXPERT_FF, D_MODEL), BF16)],
        compiler_params=_cparams(("parallel", "arbitrary")),
        name="expert_ffn")(xe, w_gate, w_up, w_down)


def _combine_kernel(final, start_ref, cnt_ref, x_ref, lr_ref, gate_ref, ye_hbm, *rest):
    if final:
        ln_ref, o_ref, slab, sems = rest
    else:
        o_ref, slab, sems = rest
    j = pl.program_id(0)
    nt = pl.num_programs(0)
    buf = lax.rem(j, 2)
    extra = 2
    lr = lr_ref[0]
    gate = gate_ref[0]
    g_hi = gate.astype(BF16).astype(F32)
    g_lo = gate - g_hi
    shifts = [_slot_geometry(start_ref, cnt_ref, j, e)[1] for e in range(N_EXPERTS)]
    rounds = _slot_rounds(start_ref, cnt_ref, j)

    def copy(jj, e, rnd, b):
        abase = _slot_geometry(start_ref, cnt_ref, jj, e)[0]
        src = pl.multiple_of(abase + rnd * MOE_SLOTS, MOE_SLOT_ALIGN)
        return pltpu.make_async_copy(ye_hbm.at[e, pl.ds(src, MOE_SLOTS)], slab.at[b, e], sems.at[b, e])

    def each(jj, rnd, fn, fn_idle=None):
        for e in range(N_EXPERTS):
            _, _, cnt, total = _slot_geometry(start_ref, cnt_ref, jj, e)
            active = (cnt > 0) & (total > rnd * MOE_SLOTS)

            @pl.when(active)
            def _():
                fn(e)

            if fn_idle is not None:
                @pl.when(jnp.logical_not(active))
                def _():
                    fn_idle(e)

    def window(rnd, b):
        w_hi = _slot_one_hot(lr, shifts, rnd, g_hi).astype(BF16)
        w_lo = _slot_one_hot(lr, shifts, rnd, g_lo).astype(BF16)

        def idle(e):
            slab[b, e] = jnp.zeros((MOE_SLOTS, D_MODEL), BF16)

        each(j, rnd, lambda e: copy(j, e, rnd, b).wait(), idle)
        ye = slab[b].reshape(N_EXPERTS * MOE_SLOTS, D_MODEL)
        r = lax.dot_general(jnp.concatenate([w_hi, w_lo], axis=1), ye, (((0,), (0,)), ((), ())),
                            preferred_element_type=F32)
        return r[:MOE_TOKEN_TILE] + r[MOE_TOKEN_TILE:]

    @pl.when(j == 0)
    def _():
        each(j, 0, lambda e: copy(j, e, 0, buf).start())

    jn = jnp.minimum(j + 1, nt - 1)

    @pl.when(j + 1 < nt)
    def _():
        each(jn, 0, lambda e: copy(jn, e, 0, 1 - buf).start())

    y = window(0, buf)

    def more(r, acc):
        each(j, r, lambda e: copy(j, e, r, extra).start())
        return acc + window(r, extra)

    y = lax.cond(rounds > 1, lambda: lax.fori_loop(1, rounds, more, y), lambda: y)
    x2 = x_ref[...] + y
    o_ref[...] = _rms(x2, ln_ref[...]) if final else x2


def _combine(base, cnt, x1, lr, gate3, ye, ln_f):
    n = x1.shape[0]
    nt = n // MOE_TOKEN_TILE
    final = ln_f is not None
    tok = pl.BlockSpec((MOE_TOKEN_TILE, D_MODEL), lambda j, b, c: (j, 0))
    sel = pl.BlockSpec((1, N_EXPERTS, MOE_TOKEN_TILE), lambda j, b, c: (j, 0, 0))
    in_specs = [tok, sel, sel, pl.BlockSpec(memory_space=pl.ANY)]
    args = [base, cnt, x1, lr, gate3, ye]
    if final:
        in_specs.append(pl.BlockSpec((1, D_MODEL), lambda j, b, c: (0, 0)))
        args.append(ln_f)
    return pl.pallas_call(
        functools.partial(_combine_kernel, final),
        grid_spec=pltpu.PrefetchScalarGridSpec(
            num_scalar_prefetch=2, grid=(nt,), in_specs=in_specs, out_specs=tok,
            scratch_shapes=[pltpu.VMEM((3, N_EXPERTS, MOE_SLOTS, D_MODEL), BF16),
                            pltpu.SemaphoreType.DMA((3, N_EXPERTS))]),
        out_shape=jax.ShapeDtypeStruct((n, D_MODEL), F32),
        compiler_params=_cparams(("arbitrary",)),
        name="moe_combine_final" if final else "moe_combine")(*args)


def _moe(x1, h_bf, aff3, layer, w_gate, w_up, w_down, ln_f):
    n = h_bf.shape[0]
    cap = EC_CAPACITY * n // N_EXPERTS
    lr, gate3, tbl = _select(aff3, cap)
    start, cnt = tbl[:, :, 0], tbl[:, :, 1]
    xe = _dispatch(start, cnt, h_bf, lr, cap)
    ye = _expert_ffn(xe, cap, layer, w_gate, w_up, w_down)
    return _combine(start, cnt, x1, lr, gate3, ye, ln_f)


def _prep_layer(l, ln1, w_in, conv_w, a_log, dt_bias, gdn_norm, rpb, w_out, ln2, w_router):
    w_pad = jnp.pad(w_in[l], ((0, 0), (0, IN_WIDTH_PAD - IN_WIDTH))).astype(BF16)
    return dict(
        ln1=ln1[l].reshape(1, D_MODEL), w_pad=w_pad,
        wgt=jnp.transpose(w_in[l][:, GATE_COL0:IN_WIDTH]).astype(BF16),
        conv_w=jnp.pad(conv_w[l], ((0, 8 - CONV_K), (0, 0))),
        a_log=a_log[l], dt_bias=dt_bias[l], gdn_norm=gdn_norm[l].reshape(1, GDN_HEAD_DIM),
        bias_tab=_na_bias_table(rpb[l]), w_out=w_out[l].astype(BF16), ln2=ln2[l].reshape(1, D_MODEL),
        w_router_t=jnp.transpose(w_router[l]))


def _trunk(x3, layers, experts, ln_f):
    batch, seq, _ = x3.shape
    x = x3.reshape(batch * seq, D_MODEL)
    for li, p in enumerate(layers):
        q, k, v, c, z, gate, gate_t = _inproj(x, p["ln1"], p["w_pad"], p["wgt"])
        na = _neighbourhood_attention(q, k, v, p["bias_tab"], batch)
        gq, gk, gv = _gdn_prep(c, p["conv_w"], batch)
        o_f, o_b = _gdn_scan(gq, gk, gv, gate, gate_t, p["a_log"], p["dt_bias"], batch)
        x1, h_bf, aff3 = _outproj(x, na, o_f, o_b, z, p["gdn_norm"], p["w_out"], p["ln2"], p["w_router_t"])
        last = li == len(layers) - 1
        x = _moe(x1, h_bf, aff3, li, *experts, ln_f.reshape(1, D_MODEL) if last else None)
    return x.reshape(batch, seq, D_MODEL)


def kernel(x_prompt, x_sample, ln1, w_in, conv_w, a_log, dt_bias, gdn_norm, rpb, w_out, ln2, w_router, w_gate, w_up, w_down, ln_f):
    layers = [_prep_layer(l, ln1, w_in, conv_w, a_log, dt_bias, gdn_norm, rpb, w_out, ln2, w_router)
              for l in range(ln1.shape[0])]
    experts = (w_gate, w_up, w_down)
    return (_trunk(x_prompt, layers, experts, ln_f), _trunk(x_sample, layers, experts, ln_f))
```
